```python
import jax, jax.numpy as jnp
from jax import lax
import numpy as np

D_MODEL = 1024
BATCH = 32
SEQ = 2048
DEPTH = 1
DEC_BATCH = 128
DEC_SEQ = 8
PAST_LEN = 8192
PAGE_SIZE = 128

POOL_WINDOWS = (2, 4, 8, 16)
N_POOL_GROUPS = len(POOL_WINDOWS)
POOL_WIDTH = D_MODEL // 2
POOL_GROUP_DIM = POOL_WIDTH // N_POOL_GROUPS
POOL_STATE = max(POOL_WINDOWS) - 1
ATTN_PATTERNS = ((128, 1), (512, 4), (2048, 16))
N_ATTN_GROUPS = len(ATTN_PATTERNS)
HEAD_DIM = 64
HEADS_PER_GROUP = 4
N_HEADS = N_ATTN_GROUPS * HEADS_PER_GROUP
QKV_WIDTH = N_HEADS * HEAD_DIM
ATTN_OUT_WIDTH = HEADS_PER_GROUP * HEAD_DIM
GATE_WIDTH = D_MODEL
IN_SPLITS = (POOL_WIDTH, POOL_WIDTH + QKV_WIDTH, POOL_WIDTH + 2 * QKV_WIDTH,
             POOL_WIDTH + 3 * QKV_WIDTH, POOL_WIDTH + 3 * QKV_WIDTH + GATE_WIDTH)
IN_WIDTH = POOL_WIDTH + 3 * QKV_WIDTH + 2 * GATE_WIDTH
D_FF = 4 * D_MODEL
QBLOCK = 128
EPS = 1e-6
F32 = jnp.float32

kernel_name = 'gated_pool_dilated_attn_decoder_step'


def rms_norm(x, w):
    x32 = x.astype(F32)
    y = x32 * lax.rsqrt(jnp.mean(jnp.square(x32), axis=-1, keepdims=True) + EPS)
    return y.astype(x.dtype) * w


def project(x, ln1, w_in, q_norm, k_norm):
    B, T, _ = x.shape
    p = jnp.einsum('btd,de->bte', rms_norm(x, ln1), w_in)
    a, q, k, v, g_a, g_b = jnp.split(p, IN_SPLITS, axis=-1)
    heads = lambda t: t.reshape(B, T, N_HEADS, HEAD_DIM)
    return a, rms_norm(heads(q), q_norm), rms_norm(heads(k), k_norm), heads(v), g_a, g_b


def causal_multiscale_pool(a, prev, pos0, lin, scale):
    B, T, _ = a.shape
    P = prev.shape[1]
    full = jnp.concatenate([prev, a], axis=1).astype(F32)
    cs = jnp.concatenate([jnp.zeros((B, 1, POOL_WIDTH), F32), lax.cumsum(full, axis=1)], axis=1)
    end = cs[:, P + 1:]
    pos = pos0 + jnp.arange(T)
    means = []
    for g, w in enumerate(POOL_WINDOWS):
        cg = slice(g * POOL_GROUP_DIM, (g + 1) * POOL_GROUP_DIM)
        start = cs[:, P + 1 - w:P + 1 - w + T, cg]
        cnt = jnp.minimum(pos + 1, w).astype(F32)[None, :, None]
        means.append((end[..., cg] - start) / cnt)
    diff = (jnp.concatenate(means, axis=-1) - a.astype(F32)).astype(a.dtype)
    z = jnp.einsum('btgc,gce->btge', diff.reshape(B, T, N_POOL_GROUPS, POOL_GROUP_DIM), lin)
    return z.reshape(B, T, POOL_WIDTH) * scale


def masked_softmax_stats(s, valid):
    s = jnp.where(valid, s, -jnp.inf)
    m = jnp.max(s, axis=-1, keepdims=True)
    p = jnp.exp(s - m)
    den = jnp.sum(p, axis=-1, keepdims=True)
    return p / den, (m + jnp.log(den))[..., 0]


def dilated_attention_prompt(q, k, v, dil, band):
    B, S, H, Dh = q.shape
    L = S // dil
    nb = -(-L // QBLOCK)
    Lp = nb * QBLOCK
    BD = B * dil

    def by_residue(x):
        x = x.reshape(B, L, dil, H, Dh).transpose(0, 2, 1, 3, 4).reshape(BD, L, H, Dh)
        return jnp.pad(x, ((0, 0), (0, Lp - L), (0, 0), (0, 0))).astype(F32)

    qr, kr, vr = by_residue(q), by_residue(k), by_residue(v)

    def band_rows(x):
        xp = jnp.pad(x, ((0, 0), (QBLOCK, 0), (0, 0), (0, 0)))
        prev = xp[:, :Lp].reshape(BD, nb, QBLOCK, H, Dh)
        cur = x.reshape(BD, nb, QBLOCK, H, Dh)
        return jnp.concatenate([prev, cur], axis=2)

    qb = qr.reshape(BD, nb, QBLOCK, H, Dh)
    kb, vb = band_rows(kr), band_rows(vr)
    s = jnp.einsum('bnqhd,bnkhd->bnhqk', qb, kb) * (HEAD_DIM ** -0.5)
    qi = jnp.arange(QBLOCK)[:, None]
    kj = jnp.arange(2 * QBLOCK)[None, :]
    dist = qi + QBLOCK - kj
    key_pos = jnp.arange(nb)[:, None, None] * QBLOCK - QBLOCK + kj[None]
    valid = (dist >= 0) & (dist <= band) & (key_pos >= 0)
    p, lse = masked_softmax_stats(s, valid[None, :, None])
    o = jnp.einsum('bnhqk,bnkhd->bnqhd', p, vb).reshape(BD, Lp, H, Dh)[:, :L]
    o = o.reshape(B, dil, L, H, Dh).transpose(0, 2, 1, 3, 4).reshape(B, S, H, Dh)
    lse = lse.transpose(0, 1, 3, 2).reshape(BD, Lp, H)[:, :L]
    lse = lse.reshape(B, dil, L, H).transpose(0, 2, 1, 3).reshape(B, S, H)
    return o, lse


def dilated_attention_sample(q, k_full, v_full, dil, band):
    B, T, H, Dh = q.shape
    Lw = k_full.shape[1] - T
    idx = Lw + jnp.arange(T)[:, None] - jnp.arange(band + 1)[None, :] * dil
    valid = idx >= 0
    idxc = jnp.maximum(idx, 0)
    kg = k_full[:, idxc].astype(F32)
    vg = v_full[:, idxc].astype(F32)
    s = jnp.einsum('bthd,btjhd->bthj', q.astype(F32), kg) * (HEAD_DIM ** -0.5)
    p, lse = masked_softmax_stats(s, valid[None, :, None, :])
    return jnp.einsum('bthj,btjhd->bthd', p, vg), lse


def combine_groups(outs, lses, dtype):
    wts = jax.nn.softmax(jnp.stack(lses, axis=0), axis=0)
    o = jnp.einsum('gbth,gbthd->bthd', wts, jnp.stack(outs, axis=0))
    B, T = o.shape[:2]
    return o.reshape(B, T, ATTN_OUT_WIDTH).astype(dtype)


def merge_and_mlp(x, a_mix, attn, g_a, g_b, w_pa, w_pb, w_o, ln2, w_up, w_down):
    branch_a = jnp.einsum('btc,cd->btd', a_mix, w_pa)
    branch_b = jnp.einsum('btc,cd->btd', attn, w_pb)
    mixed = jax.nn.sigmoid(g_a) * branch_a + jax.nn.sigmoid(g_b) * branch_b
    h = x + jnp.einsum('btd,de->bte', mixed, w_o)
    z = jnp.einsum('btd,df->btf', rms_norm(h, ln2), w_up)
    return h + jnp.einsum('btf,fd->btd', jnp.square(jax.nn.relu(z)), w_down)


def setup_inputs(seed: int = 0) -> dict:
    key = jax.random.key(seed)
    ks = jax.random.split(key, 20)
    nrm = lambda k, shape, sc: jax.random.normal(k, shape, F32) * sc
    return {
        'x_prompt': nrm(ks[0], (BATCH, SEQ, D_MODEL), 1.0),
        'x_sample': nrm(ks[1], (DEC_BATCH, DEC_SEQ, D_MODEL), 1.0),
        'state_pool': nrm(ks[2], (DEPTH, DEC_BATCH, POOL_STATE, POOL_WIDTH), 1.0),
        'cache_kv1': nrm(ks[3], (DEPTH, DEC_BATCH, min(ATTN_PATTERNS[0][0], PAST_LEN), 2, HEADS_PER_GROUP, HEAD_DIM), 1.0),
        'cache_kv2': nrm(ks[4], (DEPTH, DEC_BATCH, min(ATTN_PATTERNS[1][0], PAST_LEN), 2, HEADS_PER_GROUP, HEAD_DIM), 1.0),
        'cache_kv3': nrm(ks[5], (DEPTH, DEC_BATCH, min(ATTN_PATTERNS[2][0], PAST_LEN), 2, HEADS_PER_GROUP, HEAD_DIM), 1.0),
        'ln1': 1.0 + nrm(ks[6], (DEPTH, D_MODEL), 0.02),
        'w_in': nrm(ks[7], (DEPTH, D_MODEL, IN_WIDTH), D_MODEL ** -0.5),
        'q_norm': 1.0 + nrm(ks[8], (DEPTH, N_HEADS, HEAD_DIM), 0.02),
        'k_norm': 1.0 + nrm(ks[9], (DEPTH, N_HEADS, HEAD_DIM), 0.02),
        'pool_lin': nrm(ks[10], (DEPTH, N_POOL_GROUPS, POOL_GROUP_DIM, POOL_GROUP_DIM), POOL_GROUP_DIM ** -0.5),
        'pool_scale': 1.0 + nrm(ks[11], (DEPTH, POOL_WIDTH), 0.02),
        'w_pa': nrm(ks[12], (DEPTH, POOL_WIDTH, D_MODEL), POOL_WIDTH ** -0.5),
        'w_pb': nrm(ks[13], (DEPTH, ATTN_OUT_WIDTH, D_MODEL), ATTN_OUT_WIDTH ** -0.5),
        'w_o': nrm(ks[14], (DEPTH, D_MODEL, D_MODEL), D_MODEL ** -0.5),
        'ln2': 1.0 + nrm(ks[15], (DEPTH, D_MODEL), 0.02),
        'w_up': nrm(ks[16], (DEPTH, D_MODEL, D_FF), D_MODEL ** -0.5),
        'w_down': nrm(ks[17], (DEPTH, D_FF, D_MODEL), D_FF ** -0.5),
    }


def reference(x_prompt, x_sample, state_pool, cache_kv1, cache_kv2, cache_kv3, ln1, w_in, q_norm, k_norm,
              pool_lin, pool_scale, w_pa, w_pb, w_o, ln2, w_up, w_down):
    caches = (cache_kv1, cache_kv2, cache_kv3)
    yp, ys = x_prompt, x_sample
    pool_p, pool_s = [], []
    kv_p = [[] for _ in ATTN_PATTERNS]
    kv_s = [[] for _ in ATTN_PATTERNS]
    for l in range(DEPTH):
        a, q, k, v, g_a, g_b = project(yp, ln1[l], w_in[l], q_norm[l], k_norm[l])
        B, S = a.shape[:2]
        a_mix = causal_multiscale_pool(a, jnp.zeros((B, POOL_STATE, POOL_WIDTH), a.dtype), 0,
                                       pool_lin[l], pool_scale[l])
        pool_p.append(a[:, S - POOL_STATE:])
        outs, lses = [], []
        for g, (win, dil) in enumerate(ATTN_PATTERNS):
            hs = slice(g * HEADS_PER_GROUP, (g + 1) * HEADS_PER_GROUP)
            o, lse = dilated_attention_prompt(q[:, :, hs], k[:, :, hs], v[:, :, hs], dil, win // dil)
            outs.append(o)
            lses.append(lse)
            keep = min(win, S)
            kv_p[g].append(jnp.stack([k[:, S - keep:, hs], v[:, S - keep:, hs]], axis=2))
        yp = merge_and_mlp(yp, a_mix, combine_groups(outs, lses, yp.dtype), g_a, g_b,
                           w_pa[l], w_pb[l], w_o[l], ln2[l], w_up[l], w_down[l])

        a, q, k, v, g_a, g_b = project(ys, ln1[l], w_in[l], q_norm[l], k_norm[l])
        T = a.shape[1]
        a_mix = causal_multiscale_pool(a, state_pool[l], PAST_LEN, pool_lin[l], pool_scale[l])
        pool_s.append(jnp.concatenate([state_pool[l], a], axis=1)[:, T:])
        outs, lses = [], []
        for g, (win, dil) in enumerate(ATTN_PATTERNS):
            hs = slice(g * HEADS_PER_GROUP, (g + 1) * HEADS_PER_GROUP)
            buf = caches[g][l]
            k_full = jnp.concatenate([buf[:, :, 0], k[:, :, hs]], axis=1)
            v_full = jnp.concatenate([buf[:, :, 1], v[:, :, hs]], axis=1)
            o, lse = dilated_attention_sample(q[:, :, hs], k_full, v_full, dil, win // dil)
            outs.append(o)
            lses.append(lse)
            kv_s[g].append(jnp.stack([k_full[:, T:], v_full[:, T:]], axis=2))
        ys = merge_and_mlp(ys, a_mix, combine_groups(outs, lses, ys.dtype), g_a, g_b,
                           w_pa[l], w_pb[l], w_o[l], ln2[l], w_up[l], w_down[l])

    pool_prompt = jnp.stack(pool_p)
    kv1_prompt = jnp.stack(kv_p[0])
    kv2_prompt = jnp.stack(kv_p[1])
    kv3_prompt = jnp.stack(kv_p[2])
    pool_sample = jnp.stack(pool_s)
    kv1_sample = jnp.stack(kv_s[0])
    kv2_sample = jnp.stack(kv_s[1])
    kv3_sample = jnp.stack(kv_s[2])
    return (yp, ys, pool_prompt, kv1_prompt, kv2_prompt, kv3_prompt, pool_sample, kv1_sample, kv2_sample, kv3_sample)
```

```python
import functools

import jax
import jax.numpy as jnp
from jax import lax
from jax.experimental import pallas as pl
from jax.experimental.pallas import tpu as pltpu

F32 = jnp.float32
BF16 = jnp.bfloat16

D_MODEL = 1024
POOL_WINDOWS = (2, 4, 8, 16)
POOL_WIDTH = 512
POOL_GROUP_DIM = 128
POOL_STATE = 15
POOL_HIST = 16
ATTN_PATTERNS = ((128, 1), (512, 4), (2048, 16))
BAND = 128
HEAD_DIM = 64
HEAD_SHIFT = 6
HEADS_PER_GROUP = 4
GROUP_WIDTH = HEADS_PER_GROUP * HEAD_DIM
QKV_WIDTH = 3 * GROUP_WIDTH
N_PAIRS = QKV_WIDTH // 128
D_FF = 4096
PAST_LEN = 8192
EPS = 1e-6
COL_A, COL_Q, COL_K, COL_V, COL_GA, COL_GB = 0, 512, 1280, 2048, 2816, 3840
QBLOCK = 128
NEG = -1e30
LANES = 128
VMEM_LIMIT = 56 * 1024 * 1024


def _dot(a, b):
    return jnp.dot(a, b, preferred_element_type=F32)


def _dot_nt(a, b):
    return lax.dot_general(a, b, (((1,), (1,)), ((), ())), preferred_element_type=F32)


def _rms_norm_bf16(x, w):
    ms = jnp.mean(x * x, axis=-1, keepdims=True)
    return ((x * lax.rsqrt(ms + EPS)) * w).astype(BF16)


def _sigmoid(x):
    return 1.0 / (1.0 + jnp.exp(-x))


def _segment_ones():
    r = lax.broadcasted_iota(jnp.int32, (GROUP_WIDTH, GROUP_WIDTH), 0) >> HEAD_SHIFT
    c = lax.broadcasted_iota(jnp.int32, (GROUP_WIDTH, GROUP_WIDTH), 1) >> HEAD_SHIFT
    return jnp.where(r == c, 1.0, 0.0).astype(BF16)


def _head_norm(y, w, seg):
    sq = y * y
    hi = sq.astype(BF16)
    lo = (sq - hi.astype(F32)).astype(BF16)
    ss = _dot(hi, seg) + _dot(lo, seg)
    return (y * lax.rsqrt(ss * (1.0 / HEAD_DIM) + EPS)) * w


def _project_qkv(u, w_ref, qn_ref, kn_ref, store):
    seg = _segment_ones()
    for g in range(3):
        cs = g * GROUP_WIDTH
        q = _dot(u, w_ref[:, COL_Q + cs:COL_Q + cs + GROUP_WIDTH])
        q = _head_norm(q, qn_ref[:, cs:cs + GROUP_WIDTH], seg) * (HEAD_DIM ** -0.5)
        k = _dot(u, w_ref[:, COL_K + cs:COL_K + cs + GROUP_WIDTH])
        k = _head_norm(k, kn_ref[:, cs:cs + GROUP_WIDTH], seg)
        v = _dot(u, w_ref[:, COL_V + cs:COL_V + cs + GROUP_WIDTH])
        store(g, q, k, v)


def _pool_branch_and_gates(u, diff, w_ref, lin_ref, scale_ref, wpa_ref, ma_out, sgb_out):
    d16 = diff.astype(BF16)
    z = jnp.concatenate([_dot(d16[:, 0:256], lin_ref[0]), _dot(d16[:, 256:512], lin_ref[1])], axis=1)
    a_mix = (z * scale_ref[...]).astype(BF16)
    branch_a = _dot(a_mix, wpa_ref[...])
    g_a = _dot(u, w_ref[:, COL_GA:COL_GA + D_MODEL])
    ma_out[...] = (_sigmoid(g_a) * branch_a).astype(BF16)
    g_b = _dot(u, w_ref[:, COL_GB:COL_GB + D_MODEL])
    sgb_out[...] = _sigmoid(g_b).astype(BF16)


def _proj_prompt_kernel(x_ref, ln1_ref, w_ref, qn_ref, kn_ref, lin_ref, scale_ref, wpa_ref,
                        q_out, k_out, v_out, ma_out, sgb_out, kv1_out, kv2_out, kv3_out, pool_out,
                        abuf, *, tm):
    t = pl.program_id(1)
    last = pl.num_programs(1) - 1
    u = _rms_norm_bf16(x_ref[0], ln1_ref[...])

    def store(g, q, k, v):
        for hp in range(2):
            cols = slice(hp * LANES, (hp + 1) * LANES)
            q_out[0, 2 * g + hp] = q[:, cols]
            k_out[0, 2 * g + hp] = k[:, cols]
            v_out[0, 2 * g + hp] = v[:, cols]
        if g == 2:
            kv3_out[0, :, 0:GROUP_WIDTH] = k
            kv3_out[0, :, GROUP_WIDTH:] = v
        else:
            @pl.when(t == last)
            def _():
                if g == 1:
                    kv2_out[0, :, 0:GROUP_WIDTH] = k
                    kv2_out[0, :, GROUP_WIDTH:] = v
                else:
                    kv1_out[0, :, 0:GROUP_WIDTH] = k[tm - 128:tm]
                    kv1_out[0, :, GROUP_WIDTH:] = v[tm - 128:tm]

    _project_qkv(u, w_ref, qn_ref, kn_ref, store)

    @pl.when(t == 0)
    def _():
        abuf[0:POOL_HIST, :] = jnp.zeros((POOL_HIST, POOL_WIDTH), F32)

    abuf[POOL_HIST:POOL_HIST + tm, :] = _dot(u, w_ref[:, COL_A:COL_A + POOL_WIDTH])
    pos = t * tm + lax.broadcasted_iota(jnp.int32, (tm, 1), 0)
    diffs = []
    for g, w in enumerate(POOL_WINDOWS):
        cs = g * POOL_GROUP_DIM
        a_g = abuf[POOL_HIST:POOL_HIST + tm, cs:cs + POOL_GROUP_DIM]
        acc = a_g
        for s in range(1, w):
            acc = acc + abuf[POOL_HIST - s:POOL_HIST - s + tm, cs:cs + POOL_GROUP_DIM]
        cnt = jnp.minimum(pos + 1, w).astype(F32)
        diffs.append(acc / cnt - a_g)
    diff = jnp.concatenate(diffs, axis=1)

    @pl.when(t == last)
    def _():
        pool_out[0] = abuf[POOL_HIST + tm - POOL_STATE:POOL_HIST + tm, :]

    abuf[0:POOL_HIST, :] = abuf[tm:tm + POOL_HIST, :]
    _pool_branch_and_gates(u, diff, w_ref, lin_ref, scale_ref, wpa_ref, ma_out.at[0], sgb_out.at[0])


def _const_spec(shape):
    nd = len(shape)
    return pl.BlockSpec(shape, lambda *_: (0,) * nd, pipeline_mode=pl.Buffered(1))


def _project_prompt(x, ln1, w_in, qn, kn, lin2, scale, w_pa, *, tm=512):
    b, s, _ = x.shape
    nt = s // tm
    assert s % tm == 0 and tm == ATTN_PATTERNS[1][0] and s >= ATTN_PATTERNS[2][0]
    row = lambda width: pl.BlockSpec((1, tm, width), lambda i, j: (i, j, 0))
    tail = lambda rows, width: pl.BlockSpec((1, rows, width), lambda i, j: (i, 0, 0))
    pairs = pl.BlockSpec((1, N_PAIRS, tm, LANES), lambda i, j: (i, 0, j, 0))
    out_shape = (
        jax.ShapeDtypeStruct((b, N_PAIRS, s, LANES), F32),
        jax.ShapeDtypeStruct((b, N_PAIRS, s, LANES), F32),
        jax.ShapeDtypeStruct((b, N_PAIRS, s, LANES), F32),
        jax.ShapeDtypeStruct((b, s, D_MODEL), BF16),
        jax.ShapeDtypeStruct((b, s, D_MODEL), BF16),
        jax.ShapeDtypeStruct((b, 128, 2 * GROUP_WIDTH), F32),
        jax.ShapeDtypeStruct((b, 512, 2 * GROUP_WIDTH), F32),
        jax.ShapeDtypeStruct((b, s, 2 * GROUP_WIDTH), F32),
        jax.ShapeDtypeStruct((b, POOL_STATE, POOL_WIDTH), F32),
    )
    return pl.pallas_call(
        functools.partial(_proj_prompt_kernel, tm=tm),
        grid=(b, nt),
        in_specs=[row(D_MODEL), _const_spec(ln1.shape), _const_spec(w_in.shape), _const_spec(qn.shape),
                  _const_spec(kn.shape), _const_spec(lin2.shape), _const_spec(scale.shape),
                  _const_spec(w_pa.shape)],
        out_specs=(pairs, pairs, pairs, row(D_MODEL), row(D_MODEL),
                   tail(128, 2 * GROUP_WIDTH), tail(512, 2 * GROUP_WIDTH), row(2 * GROUP_WIDTH),
                   tail(POOL_STATE, POOL_WIDTH)),
        out_shape=out_shape,
        scratch_shapes=[pltpu.VMEM((POOL_HIST + tm, POOL_WIDTH), F32)],
        compiler_params=pltpu.CompilerParams(dimension_semantics=("arbitrary", "arbitrary"),
                                             vmem_limit_bytes=VMEM_LIMIT),
        name="project_prompt",
    )(x, ln1, w_in, qn, kn, lin2, scale, w_pa)


def _proj_sample_kernel(x_ref, st_ref, ln1_ref, w_ref, qn_ref, kn_ref, lin_ref, scale_ref, wpa_ref,
                        q_out, k_out, v_out, ma_out, sgb_out, pool_out, abuf, dbuf, *, nb, nt):
    u = jnp.concatenate([_rms_norm_bf16(x_ref[:, t * D_MODEL:(t + 1) * D_MODEL], ln1_ref[...]) for t in range(nt)],
                        axis=0)

    def store(g, q, k, v):
        cols = slice(g * GROUP_WIDTH, (g + 1) * GROUP_WIDTH)
        q_out[:, cols] = q
        k_out[:, cols] = k
        v_out[:, cols] = v

    _project_qkv(u, w_ref, qn_ref, kn_ref, store)

    abuf[...] = _dot(u, w_ref[:, COL_A:COL_A + POOL_WIDTH])

    def slab(j, c0, width):
        if j < POOL_STATE:
            return st_ref[:, j * POOL_WIDTH + c0:j * POOL_WIDTH + c0 + width]
        return abuf[(j - POOL_STATE) * nb:(j - POOL_STATE + 1) * nb, c0:c0 + width]

    for t in range(nt):
        for g, w in enumerate(POOL_WINDOWS):
            c0 = g * POOL_GROUP_DIM
            a_t = slab(POOL_STATE + t, c0, POOL_GROUP_DIM)
            acc = a_t
            for s in range(1, w):
                acc = acc + slab(POOL_STATE + t - s, c0, POOL_GROUP_DIM)
            cnt = float(min(PAST_LEN + t + 1, w))
            dbuf[t * nb:(t + 1) * nb, c0:c0 + POOL_GROUP_DIM] = acc / cnt - a_t

    for j in range(POOL_STATE):
        pool_out[:, j * POOL_WIDTH:(j + 1) * POOL_WIDTH] = slab(j + nt, 0, POOL_WIDTH)

    _pool_branch_and_gates(u, dbuf[...], w_ref, lin_ref, scale_ref, wpa_ref, ma_out, sgb_out)


def _project_sample(x_cols, state_cols, ln1, w_in, qn, kn, lin2, scale, w_pa, *, nb, nt):
    m = nb * nt
    assert nt <= POOL_STATE and x_cols.shape == (nb, nt * D_MODEL) and state_cols.shape == (nb, POOL_STATE * POOL_WIDTH)
    args = (x_cols, state_cols, ln1, w_in, qn, kn, lin2, scale, w_pa)
    out_shape = (
        jax.ShapeDtypeStruct((m, QKV_WIDTH), F32),
        jax.ShapeDtypeStruct((m, QKV_WIDTH), F32),
        jax.ShapeDtypeStruct((m, QKV_WIDTH), F32),
        jax.ShapeDtypeStruct((m, D_MODEL), BF16),
        jax.ShapeDtypeStruct((m, D_MODEL), BF16),
        jax.ShapeDtypeStruct((nb, POOL_STATE * POOL_WIDTH), F32),
    )
    return pl.pallas_call(
        functools.partial(_proj_sample_kernel, nb=nb, nt=nt),
        grid=(1,),
        in_specs=[_const_spec(a.shape) for a in args],
        out_specs=tuple(pl.BlockSpec(o.shape, lambda i, nd=len(o.shape): (0,) * nd) for o in out_shape),
        out_shape=out_shape,
        scratch_shapes=[pltpu.VMEM((m, POOL_WIDTH), F32), pltpu.VMEM((m, POOL_WIDTH), F32)],
        compiler_params=pltpu.CompilerParams(dimension_semantics=("arbitrary",), vmem_limit_bytes=VMEM_LIMIT),
        name="project_sample",
    )(*args)


def _rows(start, n, stride):
    return pl.ds(start, n) if stride == 1 else pl.ds(start, n, stride=stride)


def _attn_pair_block(q, k, v, bias2):
    nk = k.shape[0]
    lane = lax.broadcasted_iota(jnp.int32, (QBLOCK, LANES), 1)
    first = lane < HEAD_DIM
    q16 = q.astype(BF16)
    zero = jnp.zeros_like(q16)
    q2 = jnp.concatenate([jnp.where(first, q16, zero), jnp.where(first, zero, q16)], axis=0)
    s = _dot_nt(q2, k.astype(BF16)) + bias2
    m = jnp.max(s, axis=-1, keepdims=True)
    p = jnp.exp(s - m).astype(BF16)
    v_ext = jnp.concatenate([v.astype(BF16), jnp.ones((nk, LANES), BF16)], axis=1)
    o = _dot(p, v_ext)
    out = jnp.where(first, o[:QBLOCK, :LANES] / o[:QBLOCK, LANES:], o[QBLOCK:, :LANES] / o[QBLOCK:, LANES:])
    lse = jnp.where(first, m[:QBLOCK] + jnp.log(o[:QBLOCK, LANES:]), m[QBLOCK:] + jnp.log(o[QBLOCK:, LANES:]))
    return out, lse


def _band_bias(nk):
    qi = lax.broadcasted_iota(jnp.int32, (QBLOCK, nk), 0)
    kj = lax.broadcasted_iota(jnp.int32, (QBLOCK, nk), 1)
    dist = qi + (nk - QBLOCK) - kj
    b = jnp.where((dist >= 0) & (dist <= BAND), 0.0, NEG).astype(F32)
    return jnp.concatenate([b, b], axis=0)


def _attn_prompt_kernel(q_ref, k_ref, v_ref, o_ref, o1_buf, l1_buf, o2_buf, l2_buf, *, seq):
    step = pl.program_id(1)
    bias_first = _band_bias(QBLOCK)
    bias_prev = _band_bias(2 * QBLOCK)

    def block(qstart, kstart, nk, dil, bias2, emit):
        qrows = _rows(qstart, QBLOCK, dil)
        krows = _rows(kstart, nk, dil)
        for hp in range(2):
            out, lse = _attn_pair_block(q_ref[0, hp, qrows, :], k_ref[0, hp, krows, :], v_ref[0, hp, krows, :], bias2)
            emit(qrows, hp, out, lse)

    def store_to(o_buf, l_buf):
        def emit(qrows, hp, out, lse):
            o_buf[hp, qrows, :] = out
            l_buf[hp, qrows, :] = lse
        return emit

    @pl.when(step == 0)
    def _():
        dil = 16
        assert seq == dil * QBLOCK

        def body(r, c):
            block(r, r, QBLOCK, dil, bias_first, store_to(o2_buf, l2_buf))
            return c
        lax.fori_loop(0, dil, body, 0)

    @pl.when(step == 1)
    def _():
        dil = 4
        nblk = seq // (dil * QBLOCK)

        def body(r, c):
            block(r, r, QBLOCK, dil, bias_first, store_to(o1_buf, l1_buf))

            def inner(nb, c2):
                block(r + nb * (dil * QBLOCK), r + (nb - 1) * (dil * QBLOCK), 2 * QBLOCK, dil, bias_prev,
                      store_to(o1_buf, l1_buf))
                return c2
            lax.fori_loop(1, nblk, inner, 0)
            return c
        lax.fori_loop(0, dil, body, 0)

    @pl.when(step == 2)
    def _():
        def emit(qrows, hp, out0, lse0):
            lse1, lse2 = l1_buf[hp, qrows, :], l2_buf[hp, qrows, :]
            m = jnp.maximum(jnp.maximum(lse0, lse1), lse2)
            w0, w1, w2 = jnp.exp(lse0 - m), jnp.exp(lse1 - m), jnp.exp(lse2 - m)
            num = w0 * out0 + w1 * o1_buf[hp, qrows, :] + w2 * o2_buf[hp, qrows, :]
            o_ref[0, qrows, hp * LANES:(hp + 1) * LANES] = (num / (w0 + w1 + w2)).astype(o_ref.dtype)

        block(0, 0, QBLOCK, 1, bias_first, emit)

        def body(nb, c):
            start = pl.multiple_of(nb * QBLOCK, QBLOCK)
            block(start, start - QBLOCK, 2 * QBLOCK, 1, bias_prev, emit)
            return c
        lax.fori_loop(1, seq // QBLOCK, body, 0)


def _attention_prompt(q, k, v):
    b, _, s, _ = q.shape
    spec = pl.BlockSpec((1, 2, s, LANES), lambda i, j: (i, 2 - j, 0, 0))
    return pl.pallas_call(
        functools.partial(_attn_prompt_kernel, seq=s),
        grid=(b, 3),
        in_specs=[spec, spec, spec],
        out_specs=pl.BlockSpec((1, s, GROUP_WIDTH), lambda i, j: (i, 0, 0)),
        out_shape=jax.ShapeDtypeStruct((b, s, GROUP_WIDTH), BF16),
        scratch_shapes=[pltpu.VMEM((2, s, LANES), F32) for _ in range(4)],
        compiler_params=pltpu.CompilerParams(dimension_semantics=("arbitrary", "arbitrary"),
                                             vmem_limit_bytes=VMEM_LIMIT),
        name="attention_prompt",
    )(q, k, v)


def _attn_sample_kernel(q_ref, k_ref, v_ref, c1_ref, c2_ref, c3_ref, o_ref, n1_ref, n2_ref, n3_ref, *, nt):
    nrow = HEADS_PER_GROUP * nt
    nt_shift = nt.bit_length() - 1
    assert nt == 1 << nt_shift
    row_head = lax.broadcasted_iota(jnp.int32, (nrow, GROUP_WIDTH), 0) >> nt_shift
    col_head = lax.broadcasted_iota(jnp.int32, (nrow, GROUP_WIDTH), 1) >> HEAD_SHIFT
    own = row_head == col_head
    col_head_t = lax.broadcasted_iota(jnp.int32, (nt, GROUP_WIDTH), 1) >> HEAD_SHIFT

    def own_blocks(x):
        acc = jnp.zeros((nt, GROUP_WIDTH), F32)
        for h in range(HEADS_PER_GROUP):
            acc = acc + jnp.where(col_head_t == h, x[h * nt:(h + 1) * nt], 0.0)
        return acc

    outs, lses = [], []
    for g, (c_ref, n_ref) in enumerate(((c1_ref, n1_ref), (c2_ref, n2_ref), (c3_ref, n3_ref))):
        win, dil = ATTN_PATTERNS[g]
        cs = g * GROUP_WIDTH
        k_new = k_ref[:, cs:cs + GROUP_WIDTH]
        v_new = v_ref[:, cs:cs + GROUP_WIDTH]
        n_ref[0, 0:win - nt, :] = c_ref[0, nt:win, :]
        n_ref[0, win - nt:win, 0:GROUP_WIDTH] = k_new
        n_ref[0, win - nt:win, GROUP_WIDTH:] = v_new

        q = q_ref[:, cs:cs + GROUP_WIDTH]
        qbd = jnp.where(own, jnp.concatenate([q] * HEADS_PER_GROUP, axis=0), 0.0).astype(BF16)
        k_old = c_ref[0, :, 0:GROUP_WIDTH].astype(BF16)
        v_old = c_ref[0, :, GROUP_WIDTH:].astype(BF16)
        assert c_ref.shape[1] == win and dil & (dil - 1) == 0
        t_old = lax.broadcasted_iota(jnp.int32, (nrow, win), 0) & (nt - 1)
        i_old = lax.broadcasted_iota(jnp.int32, (nrow, win), 1)
        ok_old = (i_old >= t_old) & (((i_old - t_old) & (dil - 1)) == 0)
        t_new = lax.broadcasted_iota(jnp.int32, (nrow, nt), 0) & (nt - 1)
        i_new = lax.broadcasted_iota(jnp.int32, (nrow, nt), 1)
        ok_new = (i_new <= t_new) & (((t_new - i_new) & (dil - 1)) == 0)
        s_old = _dot_nt(qbd, k_old) + jnp.where(ok_old, 0.0, NEG)
        s_new = _dot_nt(qbd, k_new.astype(BF16)) + jnp.where(ok_new, 0.0, NEG)
        m = jnp.maximum(jnp.max(s_old, axis=-1, keepdims=True), jnp.max(s_new, axis=-1, keepdims=True))
        p_old = jnp.exp(s_old - m)
        p_new = jnp.exp(s_new - m)
        den = jnp.sum(p_old, axis=-1, keepdims=True) + jnp.sum(p_new, axis=-1, keepdims=True)
        o = (_dot(p_old.astype(BF16), v_old) + _dot(p_new.astype(BF16), v_new.astype(BF16))) / den
        outs.append(own_blocks(o))
        lses.append(own_blocks(jnp.broadcast_to(m + jnp.log(den), (nrow, GROUP_WIDTH))))

    m = jnp.maximum(jnp.maximum(lses[0], lses[1]), lses[2])
    ws = [jnp.exp(l - m) for l in lses]
    num = ws[0] * outs[0] + ws[1] * outs[1] + ws[2] * outs[2]
    o_ref[...] = (num / (ws[0] + ws[1] + ws[2])).astype(o_ref.dtype)


def _attention_sample(q, k, v, c1, c2, c3, *, nb, nt):
    tok = pl.BlockSpec((nt, QKV_WIDTH), lambda i: (0, i))
    cache = lambda c: pl.BlockSpec((1,) + c.shape[1:], lambda i: (i, 0, 0))
    return pl.pallas_call(
        functools.partial(_attn_sample_kernel, nt=nt),
        grid=(nb,),
        in_specs=[tok, tok, tok, cache(c1), cache(c2), cache(c3)],
        out_specs=(pl.BlockSpec((nt, GROUP_WIDTH), lambda i: (0, i)), cache(c1), cache(c2), cache(c3)),
        out_shape=(jax.ShapeDtypeStruct((nt, nb * GROUP_WIDTH), BF16),
                   jax.ShapeDtypeStruct(c1.shape, F32), jax.ShapeDtypeStruct(c2.shape, F32),
                   jax.ShapeDtypeStruct(c3.shape, F32)),
        compiler_params=pltpu.CompilerParams(dimension_semantics=("arbitrary",), vmem_limit_bytes=VMEM_LIMIT),
        name="attention_sample",
    )(q, k, v, c1, c2, c3)


def _merge_mlp_kernel(x_ref, ma_ref, sgb_ref, at_ref, wpb_ref, wo_ref, ln2_ref, wup_ref, wdn_ref, y_ref, *, ff_chunk):
    branch_b = _dot(at_ref[...], wpb_ref[...])
    mixed = (ma_ref[...].astype(F32) + sgb_ref[...].astype(F32) * branch_b).astype(BF16)
    h = x_ref[...] + _dot(mixed, wo_ref[...])
    u = _rms_norm_bf16(h, ln2_ref[...])
    acc = h
    for c in range(0, D_FF, ff_chunk):
        z = jnp.maximum(_dot(u, wup_ref[:, c:c + ff_chunk]), 0.0)
        acc = acc + _dot((z * z).astype(BF16), wdn_ref[c:c + ff_chunk, :])
    y_ref[...] = acc


def _merge_mlp(x2d, ma, sgb, attn, w_pb, w_o, ln2, w_up, w_down, *, tm=512, ff_chunk=1024, step_cols=False):
    m = ma.shape[0]
    if step_cols:
        tm = x2d.shape[0]
        xy_spec = pl.BlockSpec((tm, D_MODEL), lambda i: (0, i))
    else:
        xy_spec = pl.BlockSpec((tm, D_MODEL), lambda i: (i, 0))
    assert m % tm == 0 and x2d.size == m * D_MODEL
    row = lambda width: pl.BlockSpec((tm, width), lambda i: (i, 0))
    return pl.pallas_call(
        functools.partial(_merge_mlp_kernel, ff_chunk=ff_chunk),
        grid=(m // tm,),
        in_specs=[xy_spec, row(D_MODEL), row(D_MODEL), row(GROUP_WIDTH), _const_spec(w_pb.shape),
                  _const_spec(w_o.shape), _const_spec(ln2.shape), _const_spec(w_up.shape), _const_spec(w_down.shape)],
        out_specs=xy_spec,
        out_shape=jax.ShapeDtypeStruct(x2d.shape, F32),
        compiler_params=pltpu.CompilerParams(dimension_semantics=("arbitrary",), vmem_limit_bytes=VMEM_LIMIT),
        name="merge_mlp",
    )(x2d, ma, sgb, attn, w_pb, w_o, ln2, w_up, w_down)


def kernel(x_prompt, x_sample, state_pool, cache_kv1, cache_kv2, cache_kv3, ln1, w_in, q_norm, k_norm,
           pool_lin, pool_scale, w_pa, w_pb, w_o, ln2, w_up, w_down):
    depth = ln1.shape[0]
    assert depth == 1
    b, s, _ = x_prompt.shape
    nb, nt, _ = x_sample.shape

    w_in16, w_pa16, w_pb16 = w_in[0].astype(BF16), w_pa[0].astype(BF16), w_pb[0].astype(BF16)
    w_o16, w_up16, w_dn16 = w_o[0].astype(BF16), w_up[0].astype(BF16), w_down[0].astype(BF16)
    ln1v, ln2v = ln1[0].reshape(1, D_MODEL), ln2[0].reshape(1, D_MODEL)
    qn, kn = q_norm[0].reshape(1, QKV_WIDTH), k_norm[0].reshape(1, QKV_WIDTH)
    scale = pool_scale[0].reshape(1, POOL_WIDTH)
    lin = pool_lin[0].astype(BF16)
    zeros = jnp.zeros((POOL_GROUP_DIM, POOL_GROUP_DIM), BF16)
    lin2 = jnp.stack([jnp.block([[lin[0], zeros], [zeros, lin[1]]]), jnp.block([[lin[2], zeros], [zeros, lin[3]]])])

    q, k, v, ma, sgb, kv1, kv2, kv3, pool_p = _project_prompt(x_prompt, ln1v, w_in16, qn, kn, lin2, scale, w_pa16)
    attn = _attention_prompt(q, k, v)
    y_prompt = _merge_mlp(x_prompt.reshape(b * s, D_MODEL), ma.reshape(b * s, D_MODEL), sgb.reshape(b * s, D_MODEL),
                          attn.reshape(b * s, GROUP_WIDTH), w_pb16, w_o16, ln2v, w_up16, w_dn16).reshape(b, s, D_MODEL)

    xs = x_sample.reshape(nb, nt * D_MODEL)
    qs, ks, vs, mas, sgbs, pool_s = _project_sample(xs, state_pool[0].reshape(nb, POOL_STATE * POOL_WIDTH), ln1v, w_in16,
                                                    qn, kn, lin2, scale, w_pa16, nb=nb, nt=nt)
    caches = [c[0].reshape(nb, c.shape[2], 2 * GROUP_WIDTH) for c in (cache_kv1, cache_kv2, cache_kv3)]
    by_batch = lambda a: a.reshape(nt, nb * QKV_WIDTH)
    attn_s, n1, n2, n3 = _attention_sample(by_batch(qs), by_batch(ks), by_batch(vs), *caches, nb=nb, nt=nt)
    y_sample = _merge_mlp(xs, mas, sgbs, attn_s.reshape(nt * nb, GROUP_WIDTH), w_pb16, w_o16, ln2v, w_up16, w_dn16,
                          step_cols=True).reshape(nb, nt, D_MODEL)

    kv_shape = lambda n, w: (1, n, w, 2, HEADS_PER_GROUP, HEAD_DIM)
    return (y_prompt, y_sample,
            pool_p.reshape(1, b, POOL_STATE, POOL_WIDTH),
            kv1.reshape(kv_shape(b, 128)), kv2.reshape(kv_shape(b, 512)), kv3.reshape(kv_shape(b, s)),
            pool_s.reshape(1, nb, POOL_STATE, POOL_WIDTH),
            n1.reshape(kv_shape(nb, 128)), n2.reshape(kv_shape(nb, 512)), n3.reshape(kv_shape(nb, 2048)))
```

```python
import functools

import jax
import jax.numpy as jnp
from jax import lax
from jax.experimental import pallas as pl
from jax.experimental.pallas import tpu as pltpu

F32 = jnp.float32
BF16 = jnp.bfloat16

D_MODEL = 1024
POOL_WINDOWS = (2, 4, 8, 16)
POOL_WIDTH = 512
POOL_GROUP_DIM = 128
POOL_STATE = 15
POOL_HIST = 16
ATTN_PATTERNS = ((128, 1), (512, 4), (2048, 16))
BAND = 128
HEAD_DIM = 64
HEAD_SHIFT = 6
HEADS_PER_GROUP = 4
GROUP_WIDTH = HEADS_PER_GROUP * HEAD_DIM
KV_ROWS = 2 * GROUP_WIDTH
QKV_WIDTH = 3 * GROUP_WIDTH
N_PAIRS = QKV_WIDTH // 128
D_FF = 4096
PAST_LEN = 8192
EPS = 1e-6
COL_A, COL_Q, COL_K, COL_V, COL_GA, COL_GB = 0, 512, 1280, 2048, 2816, 3840
QBLOCK = 128
NEG = -1e30
LANES = 128
VMEM_LIMIT = 56 * 1024 * 1024


def _dot(a, b):
    return jnp.dot(a, b, preferred_element_type=F32)


def _dot_nt(a, b):
    return lax.dot_general(a, b, (((1,), (1,)), ((), ())), preferred_element_type=F32)


def _rms_norm_bf16(x, w):
    ms = jnp.mean(x * x, axis=-1, keepdims=True)
    return ((x * lax.rsqrt(ms + EPS)) * w).astype(BF16)


def _sigmoid(x):
    return 1.0 / (1.0 + jnp.exp(-x))


def _segment_ones():
    r = lax.broadcasted_iota(jnp.int32, (GROUP_WIDTH, GROUP_WIDTH), 0) >> HEAD_SHIFT
    c = lax.broadcasted_iota(jnp.int32, (GROUP_WIDTH, GROUP_WIDTH), 1) >> HEAD_SHIFT
    return jnp.where(r == c, 1.0, 0.0).astype(BF16)


def _head_norm(y, w, seg):
    sq = y * y
    hi = sq.astype(BF16)
    lo = (sq - hi.astype(F32)).astype(BF16)
    ss = _dot(hi, seg) + _dot(lo, seg)
    return (y * lax.rsqrt(ss * (1.0 / HEAD_DIM) + EPS)) * w


def _project_qkv(u, w_ref, qn_ref, kn_ref, store):
    seg = _segment_ones()
    for g in range(3):
        cs = g * GROUP_WIDTH
        q = _dot(u, w_ref[:, COL_Q + cs:COL_Q + cs + GROUP_WIDTH])
        q = _head_norm(q, qn_ref[:, cs:cs + GROUP_WIDTH], seg) * (HEAD_DIM ** -0.5)
        k = _dot(u, w_ref[:, COL_K + cs:COL_K + cs + GROUP_WIDTH])
        k = _head_norm(k, kn_ref[:, cs:cs + GROUP_WIDTH], seg)
        v = _dot(u, w_ref[:, COL_V + cs:COL_V + cs + GROUP_WIDTH])
        store(g, q, k, v)


def _pool_branch_and_gates(u, diff, w_ref, lin_ref, scale_ref, wpa_ref, ma_out, sgb_out):
    d16 = diff.astype(BF16)
    z = jnp.concatenate([_dot(d16[:, 0:256], lin_ref[0]), _dot(d16[:, 256:512], lin_ref[1])], axis=1)
    a_mix = (z * scale_ref[...]).astype(BF16)
    branch_a = _dot(a_mix, wpa_ref[...])
    g_a = _dot(u, w_ref[:, COL_GA:COL_GA + D_MODEL])
    ma_out[...] = (_sigmoid(g_a) * branch_a).astype(BF16)
    g_b = _dot(u, w_ref[:, COL_GB:COL_GB + D_MODEL])
    sgb_out[...] = _sigmoid(g_b).astype(BF16)


def _proj_prompt_kernel(x_ref, ln1_ref, w_ref, qn_ref, kn_ref, lin_ref, scale_ref, wpa_ref,
                        q_out, k_out, v_out, ma_out, sgb_out, kv1_out, kv2_out, kv3_out, pool_out,
                        abuf, *, tm):
    t = pl.program_id(1)
    last = pl.num_programs(1) - 1
    u = _rms_norm_bf16(x_ref[0], ln1_ref[...])

    def store(g, q, k, v):
        for hp in range(2):
            cols = slice(hp * LANES, (hp + 1) * LANES)
            q_out[0, 2 * g + hp] = q[:, cols]
            k_out[0, 2 * g + hp] = k[:, cols]
            v_out[0, 2 * g + hp] = v[:, cols]
        if g == 2:
            kv3_out[0, 0:GROUP_WIDTH, :] = k.T
            kv3_out[0, GROUP_WIDTH:, :] = v.T
        else:
            @pl.when(t == last)
            def _():
                if g == 1:
                    kv2_out[0, 0:GROUP_WIDTH, :] = k.T
                    kv2_out[0, GROUP_WIDTH:, :] = v.T
                else:
                    kv1_out[0, 0:GROUP_WIDTH, :] = k[tm - 128:tm].T
                    kv1_out[0, GROUP_WIDTH:, :] = v[tm - 128:tm].T

    _project_qkv(u, w_ref, qn_ref, kn_ref, store)

    @pl.when(t == 0)
    def _():
        abuf[0:POOL_HIST, :] = jnp.zeros((POOL_HIST, POOL_WIDTH), F32)

    abuf[POOL_HIST:POOL_HIST + tm, :] = _dot(u, w_ref[:, COL_A:COL_A + POOL_WIDTH])
    pos = t * tm + lax.broadcasted_iota(jnp.int32, (tm, 1), 0)
    diffs = []
    for g, w in enumerate(POOL_WINDOWS):
        cs = g * POOL_GROUP_DIM
        a_g = abuf[POOL_HIST:POOL_HIST + tm, cs:cs + POOL_GROUP_DIM]
        acc = a_g
        for s in range(1, w):
            acc = acc + abuf[POOL_HIST - s:POOL_HIST - s + tm, cs:cs + POOL_GROUP_DIM]
        cnt = jnp.minimum(pos + 1, w).astype(F32)
        diffs.append(acc / cnt - a_g)
    diff = jnp.concatenate(diffs, axis=1)

    @pl.when(t == last)
    def _():
        pool_out[0] = abuf[POOL_HIST + tm - POOL_STATE:POOL_HIST + tm, :]

    abuf[0:POOL_HIST, :] = abuf[tm:tm + POOL_HIST, :]
    _pool_branch_and_gates(u, diff, w_ref, lin_ref, scale_ref, wpa_ref, ma_out.at[0], sgb_out.at[0])


def _const_spec(shape):
    nd = len(shape)
    return pl.BlockSpec(shape, lambda *_: (0,) * nd, pipeline_mode=pl.Buffered(1))


def _project_prompt(x, ln1, w_in, qn, kn, lin2, scale, w_pa, *, tm=512):
    b, s, _ = x.shape
    nt = s // tm
    assert s % tm == 0 and tm == ATTN_PATTERNS[1][0] and s == ATTN_PATTERNS[2][0]
    row = lambda width: pl.BlockSpec((1, tm, width), lambda i, j: (i, j, 0))
    tail = lambda rows, width: pl.BlockSpec((1, rows, width), lambda i, j: (i, 0, 0))
    pairs = pl.BlockSpec((1, N_PAIRS, tm, LANES), lambda i, j: (i, 0, j, 0))
    out_shape = (
        jax.ShapeDtypeStruct((b, N_PAIRS, s, LANES), F32),
        jax.ShapeDtypeStruct((b, N_PAIRS, s, LANES), F32),
        jax.ShapeDtypeStruct((b, N_PAIRS, s, LANES), F32),
        jax.ShapeDtypeStruct((b, s, D_MODEL), BF16),
        jax.ShapeDtypeStruct((b, s, D_MODEL), BF16),
        jax.ShapeDtypeStruct((b, KV_ROWS, 128), F32),
        jax.ShapeDtypeStruct((b, KV_ROWS, 512), F32),
        jax.ShapeDtypeStruct((b, KV_ROWS, s), F32),
        jax.ShapeDtypeStruct((b, POOL_STATE, POOL_WIDTH), F32),
    )
    return pl.pallas_call(
        functools.partial(_proj_prompt_kernel, tm=tm),
        grid=(b, nt),
        in_specs=[row(D_MODEL), _const_spec(ln1.shape), _const_spec(w_in.shape), _const_spec(qn.shape),
                  _const_spec(kn.shape), _const_spec(lin2.shape), _const_spec(scale.shape),
                  _const_spec(w_pa.shape)],
        out_specs=(pairs, pairs, pairs, row(D_MODEL), row(D_MODEL),
                   tail(KV_ROWS, 128), tail(KV_ROWS, 512),
                   pl.BlockSpec((1, KV_ROWS, tm), lambda i, j: (i, 0, j)),
                   tail(POOL_STATE, POOL_WIDTH)),
        out_shape=out_shape,
        scratch_shapes=[pltpu.VMEM((POOL_HIST + tm, POOL_WIDTH), F32)],
        compiler_params=pltpu.CompilerParams(dimension_semantics=("arbitrary", "arbitrary"),
                                             vmem_limit_bytes=VMEM_LIMIT),
        name="project_prompt",
    )(x, ln1, w_in, qn, kn, lin2, scale, w_pa)


def _proj_sample_kernel(x_ref, st_ref, ln1_ref, w_ref, qn_ref, kn_ref, lin_ref, scale_ref, wpa_ref,
                        q_out, k_out, v_out, ma_out, sgb_out, pool_out, abuf, dbuf, *, nb, nt):
    u = _rms_norm_bf16(x_ref[...], ln1_ref[...])

    def store(g, q, k, v):
        cols = slice(g * GROUP_WIDTH, (g + 1) * GROUP_WIDTH)
        q_out[:, cols] = q
        k_out[:, cols] = k
        v_out[:, cols] = v

    _project_qkv(u, w_ref, qn_ref, kn_ref, store)

    a = _dot(u, w_ref[:, COL_A:COL_A + POOL_WIDTH])
    for g in range(len(POOL_WINDOWS)):
        abuf[g] = a[:, g * POOL_GROUP_DIM:(g + 1) * POOL_GROUP_DIM]

    def slab(j, g):
        if j < POOL_STATE:
            return st_ref[j, :, g * POOL_GROUP_DIM:(g + 1) * POOL_GROUP_DIM]
        return abuf[g, pl.ds(j - POOL_STATE, nb, stride=nt), :]

    for t in range(nt):
        for g, w in enumerate(POOL_WINDOWS):
            a_t = slab(POOL_STATE + t, g)
            acc = a_t
            for s in range(1, w):
                acc = acc + slab(POOL_STATE + t - s, g)
            cnt = float(min(PAST_LEN + t + 1, w))
            dbuf[g, pl.ds(t, nb, stride=nt), :] = acc / cnt - a_t

    for j in range(POOL_STATE):
        pool_out[j] = jnp.concatenate([slab(j + nt, g) for g in range(len(POOL_WINDOWS))], axis=1)

    diff = jnp.concatenate([dbuf[g] for g in range(len(POOL_WINDOWS))], axis=1)
    _pool_branch_and_gates(u, diff, w_ref, lin_ref, scale_ref, wpa_ref, ma_out, sgb_out)


def _project_sample(x2d, state_t, ln1, w_in, qn, kn, lin2, scale, w_pa, *, nb, nt):
    m = nb * nt
    assert nt <= POOL_STATE and x2d.shape == (m, D_MODEL) and state_t.shape == (POOL_STATE, nb, POOL_WIDTH)
    args = (x2d, state_t, ln1, w_in, qn, kn, lin2, scale, w_pa)
    out_shape = (
        jax.ShapeDtypeStruct((m, QKV_WIDTH), F32),
        jax.ShapeDtypeStruct((m, QKV_WIDTH), F32),
        jax.ShapeDtypeStruct((m, QKV_WIDTH), F32),
        jax.ShapeDtypeStruct((m, D_MODEL), BF16),
        jax.ShapeDtypeStruct((m, D_MODEL), BF16),
        jax.ShapeDtypeStruct((POOL_STATE, nb, POOL_WIDTH), F32),
    )
    group_buf = pltpu.VMEM((len(POOL_WINDOWS), m, POOL_GROUP_DIM), F32)
    return pl.pallas_call(
        functools.partial(_proj_sample_kernel, nb=nb, nt=nt),
        grid=(1,),
        in_specs=[_const_spec(a.shape) for a in args],
        out_specs=tuple(pl.BlockSpec(o.shape, lambda i, nd=len(o.shape): (0,) * nd) for o in out_shape),
        out_shape=out_shape,
        scratch_shapes=[group_buf, group_buf],
        compiler_params=pltpu.CompilerParams(dimension_semantics=("arbitrary",), vmem_limit_bytes=VMEM_LIMIT),
        name="project_sample",
    )(*args)


def _rows(start, n, stride):
    return pl.ds(start, n) if stride == 1 else pl.ds(start, n, stride=stride)


def _band_bias(nk, offset):
    qi = lax.broadcasted_iota(jnp.int32, (2 * QBLOCK, nk), 0) & (QBLOCK - 1)
    kj = lax.broadcasted_iota(jnp.int32, (2 * QBLOCK, nk), 1)
    dist = qi + offset - kj
    return jnp.where((dist >= 0) & (dist <= BAND), 0.0, NEG).astype(F32)


def _attn_prompt_kernel(q_ref, k_ref, v_ref, o_ref, o1_buf, l1_buf, o2_buf, l2_buf, s_buf, bias_buf, *, seq):
    step = pl.program_id(1)
    lane = lax.broadcasted_iota(jnp.int32, (QBLOCK, LANES), 1)
    first = lane < HEAD_DIM
    bias_buf[0] = _band_bias(2 * QBLOCK, 0)
    bias_buf[1] = _band_bias(2 * QBLOCK, QBLOCK)

    def scores(slot, qstart, kstart, nk, dil, bias_sel):
        qrows, krows = _rows(qstart, QBLOCK, dil), _rows(kstart, nk, dil)
        for hp in range(2):
            q16 = q_ref[0, hp, qrows, :].astype(BF16)
            zero = jnp.zeros_like(q16)
            q2 = jnp.concatenate([jnp.where(first, q16, zero), jnp.where(first, zero, q16)], axis=0)
            s = _dot_nt(q2, k_ref[0, hp, krows, :].astype(BF16))
            s_buf[slot, hp, :, 0:nk] = s + bias_buf[bias_sel, :, 0:nk]

    def values(slot, qstart, kstart, nk, dil, emit):
        qrows, krows = _rows(qstart, QBLOCK, dil), _rows(kstart, nk, dil)
        for hp in range(2):
            s = s_buf[slot, hp, :, 0:nk]
            m = jnp.max(s, axis=-1, keepdims=True)
            p = jnp.exp(s - m).astype(BF16)
            v_ext = jnp.concatenate([v_ref[0, hp, krows, :].astype(BF16), jnp.ones((nk, LANES), BF16)], axis=1)
            o = _dot(p, v_ext)
            out = jnp.where(first, o[:QBLOCK, :LANES] / o[:QBLOCK, LANES:], o[QBLOCK:, :LANES] / o[QBLOCK:, LANES:])
            lse = jnp.where(first, m[:QBLOCK] + jnp.log(o[:QBLOCK, LANES:]), m[QBLOCK:] + jnp.log(o[QBLOCK:, LANES:]))
            emit(qrows, hp, out, lse)

    def run(ntasks, task, nk, dil, emit):
        scores(0, *task(0)[:2], nk, dil, task(0)[2])

        def body(j, c):
            i0 = 2 * j
            t1, t2 = task(i0 + 1), task(jnp.minimum(i0 + 2, ntasks - 1))
            scores(1, t1[0], t1[1], nk, dil, t1[2])
            values(0, *task(i0)[:2], nk, dil, emit)
            scores(0, t2[0], t2[1], nk, dil, t2[2])
            values(1, t1[0], t1[1], nk, dil, emit)
            return c
        lax.fori_loop(0, ntasks // 2, body, 0)

    def store_to(o_buf, l_buf):
        def emit(qrows, hp, out, lse):
            o_buf[hp, qrows, :] = out
            l_buf[hp, qrows, :] = lse
        return emit

    @pl.when(step == 0)
    def _():
        dil = 16
        assert seq == dil * QBLOCK
        run(dil, lambda i: (i, i, 0), QBLOCK, dil, store_to(o2_buf, l2_buf))

    @pl.when(step == 1)
    def _():
        dil = 4
        nblk = seq // (dil * QBLOCK)
        assert nblk == 4

        def task(i):
            r, nb = i >> 2, i & 3
            prev = jnp.minimum(nb, 1)
            return r + nb * (dil * QBLOCK), r + (nb - prev) * (dil * QBLOCK), prev
        run(dil * nblk, task, 2 * QBLOCK, dil, store_to(o1_buf, l1_buf))

    @pl.when(step == 2)
    def _():
        def emit(qrows, hp, out0, lse0):
            lse1, lse2 = l1_buf[hp, qrows, :], l2_buf[hp, qrows, :]
            m = jnp.maximum(jnp.maximum(lse0, lse1), lse2)
            w0, w1, w2 = jnp.exp(lse0 - m), jnp.exp(lse1 - m), jnp.exp(lse2 - m)
            num = w0 * out0 + w1 * o1_buf[hp, qrows, :] + w2 * o2_buf[hp, qrows, :]
            o_ref[0, qrows, hp * LANES:(hp + 1) * LANES] = (num / (w0 + w1 + w2)).astype(o_ref.dtype)

        def task(i):
            prev = jnp.minimum(i, 1)
            return pl.multiple_of(jnp.int32(i) * QBLOCK, QBLOCK), pl.multiple_of((i - prev) * QBLOCK, QBLOCK), prev
        run(seq // QBLOCK, task, 2 * QBLOCK, 1, emit)


def _attention_prompt(q, k, v):
    b, _, s, _ = q.shape
    spec = pl.BlockSpec((1, 2, s, LANES), lambda i, j: (i, 2 - j, 0, 0))
    group_buf = pltpu.VMEM((2, s, LANES), F32)
    return pl.pallas_call(
        functools.partial(_attn_prompt_kernel, seq=s),
        grid=(b, 3),
        in_specs=[spec, spec, spec],
        out_specs=pl.BlockSpec((1, s, GROUP_WIDTH), lambda i, j: (i, 0, 0)),
        out_shape=jax.ShapeDtypeStruct((b, s, GROUP_WIDTH), BF16),
        scratch_shapes=[group_buf, group_buf, group_buf, group_buf,
                        pltpu.VMEM((2, 2, 2 * QBLOCK, 2 * QBLOCK), F32),
                        pltpu.VMEM((2, 2 * QBLOCK, 2 * QBLOCK), F32)],
        compiler_params=pltpu.CompilerParams(dimension_semantics=("arbitrary", "arbitrary"),
                                             vmem_limit_bytes=VMEM_LIMIT),
        name="attention_prompt",
    )(q, k, v)


def _attn_sample_kernel(q_ref, k_ref, v_ref, c1_ref, c2_ref, c3_ref, o_ref, n1_ref, n2_ref, n3_ref, *, nt):
    nrow = HEADS_PER_GROUP * nt
    nt_shift = nt.bit_length() - 1
    assert nt == 1 << nt_shift
    row_head = lax.broadcasted_iota(jnp.int32, (nrow, GROUP_WIDTH), 0) >> nt_shift
    col_head = lax.broadcasted_iota(jnp.int32, (nrow, GROUP_WIDTH), 1) >> HEAD_SHIFT
    own = row_head == col_head
    col_head_t = lax.broadcasted_iota(jnp.int32, (nt, GROUP_WIDTH), 1) >> HEAD_SHIFT
    tail_lane = lax.broadcasted_iota(jnp.int32, (KV_ROWS, LANES), 1) >= LANES - nt

    def own_blocks(x):
        acc = jnp.zeros((nt, GROUP_WIDTH), F32)
        for h in range(HEADS_PER_GROUP):
            acc = acc + jnp.where(col_head_t == h, x[h * nt:(h + 1) * nt], 0.0)
        return acc

    outs, lses = [], []
    for g, (c_ref, n_ref) in enumerate(((c1_ref, n1_ref), (c2_ref, n2_ref), (c3_ref, n3_ref))):
        win, dil = ATTN_PATTERNS[g]
        assert c_ref.shape == (1, KV_ROWS, win) and dil & (dil - 1) == 0
        cs = g * GROUP_WIDTH
        k_new = k_ref[:, cs:cs + GROUP_WIDTH]
        v_new = v_ref[:, cs:cs + GROUP_WIDTH]

        new_t = jnp.concatenate([jnp.zeros((LANES - nt, KV_ROWS), F32), jnp.concatenate([k_new, v_new], axis=1)],
                                axis=0).T
        ntiles = win // LANES
        rot = pltpu.roll(c_ref[0, :, 0:LANES], LANES - nt, axis=1)
        for j in range(ntiles):
            nxt = pltpu.roll(c_ref[0, :, (j + 1) * LANES:(j + 2) * LANES], LANES - nt, axis=1) if j + 1 < ntiles else new_t
            n_ref[0, :, j * LANES:(j + 1) * LANES] = jnp.where(tail_lane, nxt, rot)
            rot = nxt

        q = q_ref[:, cs:cs + GROUP_WIDTH]
        qbd = jnp.where(own, jnp.concatenate([q] * HEADS_PER_GROUP, axis=0), 0.0).astype(BF16)
        k_old = c_ref[0, 0:GROUP_WIDTH, :].astype(BF16)
        v_old = c_ref[0, GROUP_WIDTH:, :].astype(BF16)
        t_old = lax.broadcasted_iota(jnp.int32, (nrow, win), 0) & (nt - 1)
        i_old = lax.broadcasted_iota(jnp.int32, (nrow, win), 1)
        ok_old = (i_old >= t_old) & (((i_old - t_old) & (dil - 1)) == 0)
        t_new = lax.broadcasted_iota(jnp.int32, (nrow, nt), 0) & (nt - 1)
        i_new = lax.broadcasted_iota(jnp.int32, (nrow, nt), 1)
        ok_new = (i_new <= t_new) & (((t_new - i_new) & (dil - 1)) == 0)
        s_old = _dot(qbd, k_old) + jnp.where(ok_old, 0.0, NEG)
        s_new = _dot_nt(qbd, k_new.astype(BF16)) + jnp.where(ok_new, 0.0, NEG)
        m = jnp.maximum(jnp.max(s_old, axis=-1, keepdims=True), jnp.max(s_new, axis=-1, keepdims=True))
        p_old = jnp.exp(s_old - m)
        p_new = jnp.exp(s_new - m)
        den = jnp.sum(p_old, axis=-1, keepdims=True) + jnp.sum(p_new, axis=-1, keepdims=True)
        o = (_dot_nt(p_old.astype(BF16), v_old) + _dot(p_new.astype(BF16), v_new.astype(BF16))) / den
        outs.append(own_blocks(o))
        lses.append(own_blocks(jnp.broadcast_to(m + jnp.log(den), (nrow, GROUP_WIDTH))))

    m = jnp.maximum(jnp.maximum(lses[0], lses[1]), lses[2])
    ws = [jnp.exp(l - m) for l in lses]
    num = ws[0] * outs[0] + ws[1] * outs[1] + ws[2] * outs[2]
    o_ref[...] = num / (ws[0] + ws[1] + ws[2])


def _attention_sample(q, k, v, c1, c2, c3, *, nb, nt):
    tok = pl.BlockSpec((nt, QKV_WIDTH), lambda i: (i, 0))
    cache = lambda c: pl.BlockSpec((1,) + c.shape[1:], lambda i: (i, 0, 0))
    return pl.pallas_call(
        functools.partial(_attn_sample_kernel, nt=nt),
        grid=(nb,),
        in_specs=[tok, tok, tok, cache(c1), cache(c2), cache(c3)],
        out_specs=(pl.BlockSpec((nt, GROUP_WIDTH), lambda i: (i, 0)), cache(c1), cache(c2), cache(c3)),
        out_shape=(jax.ShapeDtypeStruct((nb * nt, GROUP_WIDTH), F32),
                   jax.ShapeDtypeStruct(c1.shape, F32), jax.ShapeDtypeStruct(c2.shape, F32),
                   jax.ShapeDtypeStruct(c3.shape, F32)),
        compiler_params=pltpu.CompilerParams(dimension_semantics=("arbitrary",), vmem_limit_bytes=VMEM_LIMIT),
        name="attention_sample",
    )(q, k, v, c1, c2, c3)


def _merge_mlp_kernel(x_ref, ma_ref, sgb_ref, at_ref, wpb_ref, wo_ref, ln2_ref, wup_ref, wdn_ref, y_ref, *, ff_chunk):
    branch_b = _dot(at_ref[...].astype(BF16), wpb_ref[...])
    mixed = (ma_ref[...].astype(F32) + sgb_ref[...].astype(F32) * branch_b).astype(BF16)
    h = x_ref[...] + _dot(mixed, wo_ref[...])
    u = _rms_norm_bf16(h, ln2_ref[...])
    acc = h
    for c in range(0, D_FF, ff_chunk):
        z = jnp.maximum(_dot(u, wup_ref[:, c:c + ff_chunk]), 0.0)
        acc = acc + _dot((z * z).astype(BF16), wdn_ref[c:c + ff_chunk, :])
    y_ref[...] = acc


def _merge_mlp(x2d, ma, sgb, attn, w_pb, w_o, ln2, w_up, w_down, *, tm=512, ff_chunk=1024):
    m = x2d.shape[0]
    tm = min(tm, m)
    assert m % tm == 0
    row = lambda width: pl.BlockSpec((tm, width), lambda i: (i, 0))
    return pl.pallas_call(
        functools.partial(_merge_mlp_kernel, ff_chunk=ff_chunk),
        grid=(m // tm,),
        in_specs=[row(D_MODEL), row(D_MODEL), row(D_MODEL), row(GROUP_WIDTH), _const_spec(w_pb.shape),
                  _const_spec(w_o.shape), _const_spec(ln2.shape), _const_spec(w_up.shape), _const_spec(w_down.shape)],
        out_specs=row(D_MODEL),
        out_shape=jax.ShapeDtypeStruct((m, D_MODEL), F32),
        compiler_params=pltpu.CompilerParams(dimension_semantics=("arbitrary",), vmem_limit_bytes=VMEM_LIMIT),
        name="merge_mlp",
    )(x2d, ma, sgb, attn, w_pb, w_o, ln2, w_up, w_down)


def _position_minor(c):
    n, w = c.shape[1], c.shape[2]
    return jnp.transpose(c[0], (0, 2, 3, 4, 1)).reshape(n, KV_ROWS, w)


def _position_major(c):
    n, _, w = c.shape
    return jnp.transpose(c.reshape(n, 2, HEADS_PER_GROUP, HEAD_DIM, w), (0, 4, 1, 2, 3))[None]


def kernel(x_prompt, x_sample, state_pool, cache_kv1, cache_kv2, cache_kv3, ln1, w_in, q_norm, k_norm,
           pool_lin, pool_scale, w_pa, w_pb, w_o, ln2, w_up, w_down):
    depth = ln1.shape[0]
    assert depth == 1
    b, s, _ = x_prompt.shape
    nb, nt, _ = x_sample.shape

    w_in16, w_pa16, w_pb16 = w_in[0].astype(BF16), w_pa[0].astype(BF16), w_pb[0].astype(BF16)
    w_o16, w_up16, w_dn16 = w_o[0].astype(BF16), w_up[0].astype(BF16), w_down[0].astype(BF16)
    ln1v, ln2v = ln1[0].reshape(1, D_MODEL), ln2[0].reshape(1, D_MODEL)
    qn, kn = q_norm[0].reshape(1, QKV_WIDTH), k_norm[0].reshape(1, QKV_WIDTH)
    scale = pool_scale[0].reshape(1, POOL_WIDTH)
    lin = pool_lin[0].astype(BF16)
    zeros = jnp.zeros((POOL_GROUP_DIM, POOL_GROUP_DIM), BF16)
    lin2 = jnp.stack([jnp.block([[lin[0], zeros], [zeros, lin[1]]]), jnp.block([[lin[2], zeros], [zeros, lin[3]]])])

    q, k, v, ma, sgb, kv1, kv2, kv3, pool_p = _project_prompt(x_prompt, ln1v, w_in16, qn, kn, lin2, scale, w_pa16)
    attn = _attention_prompt(q, k, v)
    y_prompt = _merge_mlp(x_prompt.reshape(b * s, D_MODEL), ma.reshape(b * s, D_MODEL), sgb.reshape(b * s, D_MODEL),
                          attn.reshape(b * s, GROUP_WIDTH), w_pb16, w_o16, ln2v, w_up16, w_dn16).reshape(b, s, D_MODEL)

    xs = x_sample.reshape(nb * nt, D_MODEL)
    state_t = jnp.transpose(state_pool[0], (1, 0, 2))
    qs, ks, vs, mas, sgbs, pool_s = _project_sample(xs, state_t, ln1v, w_in16, qn, kn, lin2, scale, w_pa16, nb=nb, nt=nt)
    caches = [_position_minor(c) for c in (cache_kv1, cache_kv2, cache_kv3)]
    attn_s, n1, n2, n3 = _attention_sample(qs, ks, vs, *caches, nb=nb, nt=nt)
    y_sample = _merge_mlp(xs, mas, sgbs, attn_s, w_pb16, w_o16, ln2v, w_up16, w_dn16).reshape(nb, nt, D_MODEL)

    return (y_prompt, y_sample,
            pool_p.reshape(1, b, POOL_STATE, POOL_WIDTH),
            _position_major(kv1), _position_major(kv2), _position_major(kv3),
            jnp.transpose(pool_s, (1, 0, 2))[None],
            _position_major(n1), _position_major(n2), _position_major(n3))
```

```python
import functools

import jax
import jax.numpy as jnp
from jax import lax
from jax.experimental import pallas as pl
from jax.experimental.pallas import tpu as pltpu

F32 = jnp.float32
BF16 = jnp.bfloat16

D_MODEL = 1024
POOL_WINDOWS = (2, 4, 8, 16)
POOL_WIDTH = 512
POOL_GROUP_DIM = 128
POOL_STATE = 15
POOL_HIST = 16
POOL_PAD = 8
ATTN_PATTERNS = ((128, 1), (512, 4), (2048, 16))
BAND = 128
HEAD_DIM = 64
HEAD_SHIFT = 6
HEADS_PER_GROUP = 4
GROUP_WIDTH = HEADS_PER_GROUP * HEAD_DIM
KV_ROWS = 2 * GROUP_WIDTH
QKV_WIDTH = 3 * GROUP_WIDTH
N_PAIRS = QKV_WIDTH // 128
D_FF = 4096
PAST_LEN = 8192
EPS = 1e-6
COL_A, COL_Q, COL_K, COL_V, COL_GA, COL_GB = 0, 512, 1280, 2048, 2816, 3840
Q_SCALE = HEAD_DIM ** -0.5 * 1.4426950408889634
QBLOCK = 128
NEG = -1e30
LANES = 128
VMEM_LIMIT = 56 * 1024 * 1024


def _dot(a, b):
    return jnp.dot(a, b, preferred_element_type=F32)


def _dot_nt(a, b):
    return lax.dot_general(a, b, (((1,), (1,)), ((), ())), preferred_element_type=F32)


def _rms_norm_bf16(x, w):
    ms = jnp.mean(x * x, axis=-1, keepdims=True)
    return ((x * lax.rsqrt(ms + EPS)) * w).astype(BF16)


def _sigmoid(x):
    return 1.0 / (1.0 + jnp.exp(-x))


def _segment_mean():
    r = (lax.broadcasted_iota(jnp.int32, (2 * GROUP_WIDTH, GROUP_WIDTH), 0) & (GROUP_WIDTH - 1)) >> HEAD_SHIFT
    c = lax.broadcasted_iota(jnp.int32, (2 * GROUP_WIDTH, GROUP_WIDTH), 1) >> HEAD_SHIFT
    return jnp.where(r == c, 1.0 / HEAD_DIM, 0.0).astype(BF16)


def _head_norm(y, w, seg):
    outs = []
    for g in range(3):
        cols = slice(g * GROUP_WIDTH, (g + 1) * GROUP_WIDTH)
        yg = y[:, cols]
        sq = yg * yg
        hi = sq.astype(BF16)
        lo = (sq - hi.astype(F32)).astype(BF16)
        ms = _dot(jnp.concatenate([hi, lo], axis=1), seg)
        outs.append((yg * lax.rsqrt(ms + EPS)) * w[:, cols])
    return outs


def _project_qkv(u, w_ref, qn_ref, kn_ref, store):
    seg = _segment_mean()
    q = _head_norm(_dot(u, w_ref[:, COL_Q:COL_Q + QKV_WIDTH]), qn_ref[...] * Q_SCALE, seg)
    k = _head_norm(_dot(u, w_ref[:, COL_K:COL_K + QKV_WIDTH]), kn_ref[...], seg)
    v = _dot(u, w_ref[:, COL_V:COL_V + QKV_WIDTH])
    for g in range(3):
        store(g, q[g], k[g], v[:, g * GROUP_WIDTH:(g + 1) * GROUP_WIDTH])


def _pool_branch_and_gates(u, diff, w_ref, lin_ref, scale_ref, wpa_ref, ma_out, sgb_out):
    d16 = diff.astype(BF16)
    z = jnp.concatenate([_dot(d16[:, 0:256], lin_ref[0]), _dot(d16[:, 256:512], lin_ref[1])], axis=1)
    a_mix = (z * scale_ref[...]).astype(BF16)
    branch_a = _dot(a_mix, wpa_ref[...])
    g_a = _dot(u, w_ref[:, COL_GA:COL_GA + D_MODEL])
    ma_out[...] = (_sigmoid(g_a) * branch_a).astype(BF16)
    g_b = _dot(u, w_ref[:, COL_GB:COL_GB + D_MODEL])
    sgb_out[...] = _sigmoid(g_b).astype(BF16)


def _proj_prompt_kernel(x_ref, ln1_ref, w_ref, qn_ref, kn_ref, lin_ref, scale_ref, wpa_ref,
                        q_out, k_out, v_out, ma_out, sgb_out, kv1_out, kv2_out, kv3_out, pool_out,
                        lvl, *, tm):
    t = pl.program_id(1)
    last = pl.num_programs(1) - 1
    u = _rms_norm_bf16(x_ref[0], ln1_ref[...])

    def store(g, q, k, v):
        for hp in range(2):
            cols = slice(hp * LANES, (hp + 1) * LANES)
            q_out[0, 2 * g + hp] = q[:, cols]
            k_out[0, 2 * g + hp] = k[:, cols]
            v_out[0, 2 * g + hp] = v[:, cols]
        if g == 2:
            kv3_out[0, 0:GROUP_WIDTH, :] = k.T
            kv3_out[0, GROUP_WIDTH:, :] = v.T
        else:
            @pl.when(t == last)
            def _():
                if g == 1:
                    kv2_out[0, 0:GROUP_WIDTH, :] = k.T
                    kv2_out[0, GROUP_WIDTH:, :] = v.T
                else:
                    kv1_out[0, 0:GROUP_WIDTH, :] = k[tm - 128:tm].T
                    kv1_out[0, GROUP_WIDTH:, :] = v[tm - 128:tm].T

    _project_qkv(u, w_ref, qn_ref, kn_ref, store)

    nlev = len(POOL_WINDOWS)
    top = POOL_PAD + POOL_HIST

    @pl.when(t == 0)
    def _():
        lvl[:, 0:top, :] = jnp.zeros((nlev, top, POOL_WIDTH), F32)

    lvl[0, top:top + tm, :] = _dot(u, w_ref[:, COL_A:COL_A + POOL_WIDTH])
    pos = t * tm + lax.broadcasted_iota(jnp.int32, (tm, 1), 0)
    diffs = []
    for j, w in enumerate(POOL_WINDOWS):
        assert w == 2 << j and w // 2 <= POOL_PAD
        c0 = j * POOL_GROUP_DIM
        rows = slice(POOL_PAD, top + tm)
        prev = slice(POOL_PAD - w // 2, top + tm - w // 2)
        acc = lvl[j, rows, c0:] + lvl[j, prev, c0:]
        if j + 1 < nlev:
            lvl[j + 1, rows, c0 + POOL_GROUP_DIM:] = acc[:, POOL_GROUP_DIM:]
        cnt = jnp.minimum(pos + 1, w).astype(F32)
        diffs.append(acc[POOL_HIST:, 0:POOL_GROUP_DIM] / cnt - lvl[0, top:top + tm, c0:c0 + POOL_GROUP_DIM])
    diff = jnp.concatenate(diffs, axis=1)

    @pl.when(t == last)
    def _():
        pool_out[0] = lvl[0, top + tm - POOL_STATE:top + tm, :]

    lvl[0, POOL_PAD:top, :] = lvl[0, tm + POOL_PAD:tm + top, :]
    _pool_branch_and_gates(u, diff, w_ref, lin_ref, scale_ref, wpa_ref, ma_out.at[0], sgb_out.at[0])


def _const_spec(shape):
    nd = len(shape)
    return pl.BlockSpec(shape, lambda *_: (0,) * nd, pipeline_mode=pl.Buffered(1))


def _project_prompt(x, ln1, w_in, qn, kn, lin2, scale, w_pa, *, tm=512):
    b, s, _ = x.shape
    nt = s // tm
    assert s % tm == 0 and tm == ATTN_PATTERNS[1][0] and s == ATTN_PATTERNS[2][0]
    row = lambda width: pl.BlockSpec((1, tm, width), lambda i, j: (i, j, 0))
    tail = lambda rows, width: pl.BlockSpec((1, rows, width), lambda i, j: (i, 0, 0))
    pairs = pl.BlockSpec((1, N_PAIRS, tm, LANES), lambda i, j: (i, 0, j, 0))
    out_shape = (
        jax.ShapeDtypeStruct((b, N_PAIRS, s, LANES), F32),
        jax.ShapeDtypeStruct((b, N_PAIRS, s, LANES), F32),
        jax.ShapeDtypeStruct((b, N_PAIRS, s, LANES), F32),
        jax.ShapeDtypeStruct((b, s, D_MODEL), BF16),
        jax.ShapeDtypeStruct((b, s, D_MODEL), BF16),
        jax.ShapeDtypeStruct((b, KV_ROWS, 128), F32),
        jax.ShapeDtypeStruct((b, KV_ROWS, 512), F32),
        jax.ShapeDtypeStruct((b, KV_ROWS, s), F32),
        jax.ShapeDtypeStruct((b, POOL_STATE, POOL_WIDTH), F32),
    )
    return pl.pallas_call(
        functools.partial(_proj_prompt_kernel, tm=tm),
        grid=(b, nt),
        in_specs=[row(D_MODEL), _const_spec(ln1.shape), _const_spec(w_in.shape), _const_spec(qn.shape),
                  _const_spec(kn.shape), _const_spec(lin2.shape), _const_spec(scale.shape),
                  _const_spec(w_pa.shape)],
        out_specs=(pairs, pairs, pairs, row(D_MODEL), row(D_MODEL),
                   tail(KV_ROWS, 128), tail(KV_ROWS, 512),
                   pl.BlockSpec((1, KV_ROWS, tm), lambda i, j: (i, 0, j)),
                   tail(POOL_STATE, POOL_WIDTH)),
        out_shape=out_shape,
        scratch_shapes=[pltpu.VMEM((len(POOL_WINDOWS), POOL_PAD + POOL_HIST + tm, POOL_WIDTH), F32)],
        compiler_params=pltpu.CompilerParams(dimension_semantics=("arbitrary", "arbitrary"),
                                             vmem_limit_bytes=VMEM_LIMIT),
        name="project_prompt",
    )(x, ln1, w_in, qn, kn, lin2, scale, w_pa)


def _proj_sample_kernel(x_ref, st_ref, ln1_ref, w_ref, qn_ref, kn_ref, lin_ref, scale_ref, wpa_ref,
                        q_out, k_out, v_out, ma_out, sgb_out, pool_out, abuf, dbuf, *, nb, nt):
    u = _rms_norm_bf16(x_ref[...], ln1_ref[...])

    def store(g, q, k, v):
        cols = slice(g * GROUP_WIDTH, (g + 1) * GROUP_WIDTH)
        q_out[:, cols] = q
        k_out[:, cols] = k
        v_out[:, cols] = v

    _project_qkv(u, w_ref, qn_ref, kn_ref, store)

    a = _dot(u, w_ref[:, COL_A:COL_A + POOL_WIDTH])
    for g in range(len(POOL_WINDOWS)):
        abuf[g] = a[:, g * POOL_GROUP_DIM:(g + 1) * POOL_GROUP_DIM]

    def slab(j, g):
        if j < POOL_STATE:
            return st_ref[j, :, g * POOL_GROUP_DIM:(g + 1) * POOL_GROUP_DIM]
        return abuf[g, pl.ds(j - POOL_STATE, nb, stride=nt), :]

    for t in range(nt):
        for g, w in enumerate(POOL_WINDOWS):
            a_t = slab(POOL_STATE + t, g)
            acc = a_t
            for s in range(1, w):
                acc = acc + slab(POOL_STATE + t - s, g)
            cnt = float(min(PAST_LEN + t + 1, w))
            dbuf[g, pl.ds(t, nb, stride=nt), :] = acc / cnt - a_t

    for j in range(POOL_STATE):
        pool_out[j] = jnp.concatenate([slab(j + nt, g) for g in range(len(POOL_WINDOWS))], axis=1)

    diff = jnp.concatenate([dbuf[g] for g in range(len(POOL_WINDOWS))], axis=1)
    _pool_branch_and_gates(u, diff, w_ref, lin_ref, scale_ref, wpa_ref, ma_out, sgb_out)


def _project_sample(x2d, state_t, ln1, w_in, qn, kn, lin2, scale, w_pa, *, nb, nt):
    m = nb * nt
    assert nt <= POOL_STATE and x2d.shape == (m, D_MODEL) and state_t.shape == (POOL_STATE, nb, POOL_WIDTH)
    args = (x2d, state_t, ln1, w_in, qn, kn, lin2, scale, w_pa)
    out_shape = (
        jax.ShapeDtypeStruct((m, QKV_WIDTH), F32),
        jax.ShapeDtypeStruct((m, QKV_WIDTH), F32),
        jax.ShapeDtypeStruct((m, QKV_WIDTH), F32),
        jax.ShapeDtypeStruct((m, D_MODEL), BF16),
        jax.ShapeDtypeStruct((m, D_MODEL), BF16),
        jax.ShapeDtypeStruct((POOL_STATE, nb, POOL_WIDTH), F32),
    )
    group_buf = pltpu.VMEM((len(POOL_WINDOWS), m, POOL_GROUP_DIM), F32)
    return pl.pallas_call(
        functools.partial(_proj_sample_kernel, nb=nb, nt=nt),
        grid=(1,),
        in_specs=[_const_spec(a.shape) for a in args],
        out_specs=tuple(pl.BlockSpec(o.shape, lambda i, nd=len(o.shape): (0,) * nd) for o in out_shape),
        out_shape=out_shape,
        scratch_shapes=[group_buf, group_buf],
        compiler_params=pltpu.CompilerParams(dimension_semantics=("arbitrary",), vmem_limit_bytes=VMEM_LIMIT),
        name="project_sample",
    )(*args)


def _rows(start, n, stride):
    return pl.ds(start, n) if stride == 1 else pl.ds(start, n, stride=stride)


def _band_bias(nk, offset):
    qi = lax.broadcasted_iota(jnp.int32, (2 * QBLOCK, nk), 0) & (QBLOCK - 1)
    kj = lax.broadcasted_iota(jnp.int32, (2 * QBLOCK, nk), 1)
    dist = qi + offset - kj
    return jnp.where((dist >= 0) & (dist <= BAND), 0.0, NEG).astype(F32)


def _attn_prompt_kernel(q_ref, k_ref, v_ref, o_ref, o1_buf, l1_buf, o2_buf, l2_buf, s_buf, bias_buf, *, seq):
    step = pl.program_id(1)
    lane = lax.broadcasted_iota(jnp.int32, (QBLOCK, LANES), 1)
    first = lane < HEAD_DIM
    bias_buf[0] = _band_bias(2 * QBLOCK, 0)
    bias_buf[1] = _band_bias(2 * QBLOCK, QBLOCK)

    def scores(slot, hp, qstart, kstart, nk, dil, bias_sel):
        qrows, krows = _rows(qstart, QBLOCK, dil), _rows(kstart, nk, dil)
        q16 = q_ref[0, hp, qrows, :].astype(BF16)
        zero = jnp.zeros_like(q16)
        q2 = jnp.concatenate([jnp.where(first, q16, zero), jnp.where(first, zero, q16)], axis=0)
        s = _dot_nt(q2, k_ref[0, hp, krows, :].astype(BF16))
        s_buf[slot, hp, :, 0:nk] = s + bias_buf[bias_sel, :, 0:nk]

    def values(slot, hp, qstart, kstart, nk, dil, emit):
        qrows, krows = _rows(qstart, QBLOCK, dil), _rows(kstart, nk, dil)
        s = s_buf[slot, hp, :, 0:nk]
        m = jnp.max(s, axis=-1, keepdims=True)
        p = jnp.exp2(s - m).astype(BF16)
        v_ext = jnp.concatenate([v_ref[0, hp, krows, :].astype(BF16), jnp.ones((nk, LANES), BF16)], axis=1)
        o = _dot(p, v_ext)
        num = jnp.where(first, o[:QBLOCK, :LANES], o[QBLOCK:, :LANES])
        den = jnp.where(first, o[:QBLOCK, LANES:], o[QBLOCK:, LANES:])
        lse = jnp.where(first, m[:QBLOCK], m[QBLOCK:]) + jnp.log2(den)
        emit(qrows, hp, num / den, lse)

    def run(ntasks, task, nk, dil, emit):
        assert ntasks % 4 == 0

        def stage(score_slot, i_score, value_slot, i_value):
            for d in range(2):
                sq, sk, sel = task(jnp.minimum(i_score + d, ntasks - 1))
                vq, vk, _ = task(i_value + d)
                for hp in range(2):
                    scores(score_slot + d, hp, sq, sk, nk, dil, sel)
                    values(value_slot + d, hp, vq, vk, nk, dil, emit)

        for d in range(2):
            for hp in range(2):
                scores(d, hp, *task(d)[:2], nk, dil, task(d)[2])

        def body(j, c):
            i0 = 4 * j
            stage(2, i0 + 2, 0, i0)
            stage(0, i0 + 4, 2, i0 + 2)
            return c
        lax.fori_loop(0, ntasks // 4, body, 0)

    def store_to(o_buf, l_buf):
        def emit(qrows, hp, out, lse):
            o_buf[hp, qrows, :] = out
            l_buf[hp, qrows, :] = lse
        return emit

    @pl.when(step == 0)
    def _():
        dil = 16
        assert seq == dil * QBLOCK
        run(dil, lambda i: (i, i, 0), QBLOCK, dil, store_to(o2_buf, l2_buf))

    @pl.when(step == 1)
    def _():
        dil = 4
        nblk = seq // (dil * QBLOCK)
        assert nblk == 4

        def task(i):
            r, nb = i >> 2, i & 3
            prev = jnp.minimum(nb, 1)
            return r + nb * (dil * QBLOCK), r + (nb - prev) * (dil * QBLOCK), prev
        run(dil * nblk, task, 2 * QBLOCK, dil, store_to(o1_buf, l1_buf))

    @pl.when(step == 2)
    def _():
        def emit(qrows, hp, out0, lse0):
            lse1, lse2 = l1_buf[hp, qrows, :], l2_buf[hp, qrows, :]
            m = jnp.maximum(jnp.maximum(lse0, lse1), lse2)
            w0, w1, w2 = jnp.exp2(lse0 - m), jnp.exp2(lse1 - m), jnp.exp2(lse2 - m)
            num = w0 * out0 + w1 * o1_buf[hp, qrows, :] + w2 * o2_buf[hp, qrows, :]
            o_ref[0, qrows, hp * LANES:(hp + 1) * LANES] = (num / (w0 + w1 + w2)).astype(o_ref.dtype)

        def task(i):
            prev = jnp.minimum(i, 1)
            return pl.multiple_of(jnp.int32(i) * QBLOCK, QBLOCK), pl.multiple_of((i - prev) * QBLOCK, QBLOCK), prev
        run(seq // QBLOCK, task, 2 * QBLOCK, 1, emit)


def _attention_prompt(q, k, v):
    b, _, s, _ = q.shape
    spec = pl.BlockSpec((1, 2, s, LANES), lambda i, j: (i, 2 - j, 0, 0))
    group_buf = pltpu.VMEM((2, s, LANES), F32)
    return pl.pallas_call(
        functools.partial(_attn_prompt_kernel, seq=s),
        grid=(b, 3),
        in_specs=[spec, spec, spec],
        out_specs=pl.BlockSpec((1, s, GROUP_WIDTH), lambda i, j: (i, 0, 0)),
        out_shape=jax.ShapeDtypeStruct((b, s, GROUP_WIDTH), BF16),
        scratch_shapes=[group_buf, group_buf, group_buf, group_buf,
                        pltpu.VMEM((4, 2, 2 * QBLOCK, 2 * QBLOCK), F32),
                        pltpu.VMEM((2, 2 * QBLOCK, 2 * QBLOCK), F32)],
        compiler_params=pltpu.CompilerParams(dimension_semantics=("arbitrary", "arbitrary"),
                                             vmem_limit_bytes=VMEM_LIMIT),
        name="attention_prompt",
    )(q, k, v)


def _attn_sample_kernel(q_ref, k_ref, v_ref, c1_ref, c2_ref, c3_ref, o_ref, n1_ref, n2_ref, n3_ref, *, nt):
    nrow = HEADS_PER_GROUP * nt
    nt_shift = nt.bit_length() - 1
    assert nt == 1 << nt_shift
    row_head = lax.broadcasted_iota(jnp.int32, (nrow, GROUP_WIDTH), 0) >> nt_shift
    col_head = lax.broadcasted_iota(jnp.int32, (nrow, GROUP_WIDTH), 1) >> HEAD_SHIFT
    own = row_head == col_head
    col_head_t = lax.broadcasted_iota(jnp.int32, (nt, GROUP_WIDTH), 1) >> HEAD_SHIFT
    tail_lane = lax.broadcasted_iota(jnp.int32, (KV_ROWS, LANES), 1) >= LANES - nt

    def own_blocks(x):
        acc = jnp.zeros((nt, GROUP_WIDTH), F32)
        for h in range(HEADS_PER_GROUP):
            acc = acc + jnp.where(col_head_t == h, x[h * nt:(h + 1) * nt], 0.0)
        return acc

    outs, lses = [], []
    for g, (c_ref, n_ref) in enumerate(((c1_ref, n1_ref), (c2_ref, n2_ref), (c3_ref, n3_ref))):
        win, dil = ATTN_PATTERNS[g]
        assert c_ref.shape == (1, KV_ROWS, win) and dil & (dil - 1) == 0
        cs = g * GROUP_WIDTH
        k_new = k_ref[:, cs:cs + GROUP_WIDTH]
        v_new = v_ref[:, cs:cs + GROUP_WIDTH]

        new_t = jnp.concatenate([jnp.zeros((LANES - nt, KV_ROWS), F32), jnp.concatenate([k_new, v_new], axis=1)],
                                axis=0).T
        ntiles = win // LANES
        rot = pltpu.roll(c_ref[0, :, 0:LANES], LANES - nt, axis=1)
        for j in range(ntiles):
            nxt = pltpu.roll(c_ref[0, :, (j + 1) * LANES:(j + 2) * LANES], LANES - nt, axis=1) if j + 1 < ntiles else new_t
            n_ref[0, :, j * LANES:(j + 1) * LANES] = jnp.where(tail_lane, nxt, rot)
            rot = nxt

        q = q_ref[:, cs:cs + GROUP_WIDTH]
        qbd = jnp.where(own, jnp.concatenate([q] * HEADS_PER_GROUP, axis=0), 0.0).astype(BF16)
        k_old = c_ref[0, 0:GROUP_WIDTH, :].astype(BF16)
        v_old = c_ref[0, GROUP_WIDTH:, :].astype(BF16)
        t_old = lax.broadcasted_iota(jnp.int32, (nrow, win), 0) & (nt - 1)
        i_old = lax.broadcasted_iota(jnp.int32, (nrow, win), 1)
        ok_old = (i_old >= t_old) & (((i_old - t_old) & (dil - 1)) == 0)
        t_new = lax.broadcasted_iota(jnp.int32, (nrow, nt), 0) & (nt - 1)
        i_new = lax.broadcasted_iota(jnp.int32, (nrow, nt), 1)
        ok_new = (i_new <= t_new) & (((t_new - i_new) & (dil - 1)) == 0)
        s_old = _dot(qbd, k_old) + jnp.where(ok_old, 0.0, NEG)
        s_new = _dot_nt(qbd, k_new.astype(BF16)) + jnp.where(ok_new, 0.0, NEG)
        m = jnp.maximum(jnp.max(s_old, axis=-1, keepdims=True), jnp.max(s_new, axis=-1, keepdims=True))
        p_old = jnp.exp2(s_old - m)
        p_new = jnp.exp2(s_new - m)
        den = jnp.sum(p_old, axis=-1, keepdims=True) + jnp.sum(p_new, axis=-1, keepdims=True)
        o = (_dot_nt(p_old.astype(BF16), v_old) + _dot(p_new.astype(BF16), v_new.astype(BF16))) / den
        outs.append(own_blocks(o))
        lses.append(own_blocks(jnp.broadcast_to(m + jnp.log2(den), (nrow, GROUP_WIDTH))))

    m = jnp.maximum(jnp.maximum(lses[0], lses[1]), lses[2])
    ws = [jnp.exp2(l - m) for l in lses]
    num = ws[0] * outs[0] + ws[1] * outs[1] + ws[2] * outs[2]
    o_ref[...] = num / (ws[0] + ws[1] + ws[2])


def _attention_sample(q, k, v, c1, c2, c3, *, nb, nt):
    tok = pl.BlockSpec((nt, QKV_WIDTH), lambda i: (i, 0))
    cache = lambda c: pl.BlockSpec((1,) + c.shape[1:], lambda i: (i, 0, 0))
    return pl.pallas_call(
        functools.partial(_attn_sample_kernel, nt=nt),
        grid=(nb,),
        in_specs=[tok, tok, tok, cache(c1), cache(c2), cache(c3)],
        out_specs=(pl.BlockSpec((nt, GROUP_WIDTH), lambda i: (i, 0)), cache(c1), cache(c2), cache(c3)),
        out_shape=(jax.ShapeDtypeStruct((nb * nt, GROUP_WIDTH), F32),
                   jax.ShapeDtypeStruct(c1.shape, F32), jax.ShapeDtypeStruct(c2.shape, F32),
                   jax.ShapeDtypeStruct(c3.shape, F32)),
        compiler_params=pltpu.CompilerParams(dimension_semantics=("arbitrary",), vmem_limit_bytes=VMEM_LIMIT),
        name="attention_sample",
    )(q, k, v, c1, c2, c3)


def _merge_mlp_kernel(x_ref, ma_ref, sgb_ref, at_ref, wpb_ref, wo_ref, ln2_ref, wup_ref, wdn_ref, y_ref, *, ff_chunk):
    branch_b = _dot(at_ref[...].astype(BF16), wpb_ref[...])
    mixed = (ma_ref[...].astype(F32) + sgb_ref[...].astype(F32) * branch_b).astype(BF16)
    h = x_ref[...] + _dot(mixed, wo_ref[...])
    u = _rms_norm_bf16(h, ln2_ref[...])
    acc = h
    for c in range(0, D_FF, ff_chunk):
        z = jnp.maximum(_dot(u, wup_ref[:, c:c + ff_chunk]), 0.0)
        acc = acc + _dot((z * z).astype(BF16), wdn_ref[c:c + ff_chunk, :])
    y_ref[...] = acc


def _merge_mlp(x2d, ma, sgb, attn, w_pb, w_o, ln2, w_up, w_down, *, tm=512, ff_chunk=1024):
    m = x2d.shape[0]
    tm = min(tm, m)
    assert m % tm == 0
    row = lambda width: pl.BlockSpec((tm, width), lambda i: (i, 0))
    return pl.pallas_call(
        functools.partial(_merge_mlp_kernel, ff_chunk=ff_chunk),
        grid=(m // tm,),
        in_specs=[row(D_MODEL), row(D_MODEL), row(D_MODEL), row(GROUP_WIDTH), _const_spec(w_pb.shape),
                  _const_spec(w_o.shape), _const_spec(ln2.shape), _const_spec(w_up.shape), _const_spec(w_down.shape)],
        out_specs=row(D_MODEL),
        out_shape=jax.ShapeDtypeStruct((m, D_MODEL), F32),
        compiler_params=pltpu.CompilerParams(dimension_semantics=("arbitrary",), vmem_limit_bytes=VMEM_LIMIT),
        name="merge_mlp",
    )(x2d, ma, sgb, attn, w_pb, w_o, ln2, w_up, w_down)


def _position_minor(c):
    n, w = c.shape[1], c.shape[2]
    return jnp.transpose(c[0], (0, 2, 3, 4, 1)).reshape(n, KV_ROWS, w)


def _position_major(c):
    n, _, w = c.shape
    return jnp.transpose(c.reshape(n, 2, HEADS_PER_GROUP, HEAD_DIM, w), (0, 4, 1, 2, 3))[None]


def kernel(x_prompt, x_sample, state_pool, cache_kv1, cache_kv2, cache_kv3, ln1, w_in, q_norm, k_norm,
           pool_lin, pool_scale, w_pa, w_pb, w_o, ln2, w_up, w_down):
    depth = ln1.shape[0]
    assert depth == 1
    b, s, _ = x_prompt.shape
    nb, nt, _ = x_sample.shape

    w_in16, w_pa16, w_pb16 = w_in[0].astype(BF16), w_pa[0].astype(BF16), w_pb[0].astype(BF16)
    w_o16, w_up16, w_dn16 = w_o[0].astype(BF16), w_up[0].astype(BF16), w_down[0].astype(BF16)
    ln1v, ln2v = ln1[0].reshape(1, D_MODEL), ln2[0].reshape(1, D_MODEL)
    qn, kn = q_norm[0].reshape(1, QKV_WIDTH), k_norm[0].reshape(1, QKV_WIDTH)
    scale = pool_scale[0].reshape(1, POOL_WIDTH)
    lin = pool_lin[0].astype(BF16)
    zeros = jnp.zeros((POOL_GROUP_DIM, POOL_GROUP_DIM), BF16)
    lin2 = jnp.stack([jnp.block([[lin[0], zeros], [zeros, lin[1]]]), jnp.block([[lin[2], zeros], [zeros, lin[3]]])])

    q, k, v, ma, sgb, kv1, kv2, kv3, pool_p = _project_prompt(x_prompt, ln1v, w_in16, qn, kn, lin2, scale, w_pa16)
    attn = _attention_prompt(q, k, v)
    y_prompt = _merge_mlp(x_prompt.reshape(b * s, D_MODEL), ma.reshape(b * s, D_MODEL), sgb.reshape(b * s, D_MODEL),
                          attn.reshape(b * s, GROUP_WIDTH), w_pb16, w_o16, ln2v, w_up16, w_dn16).reshape(b, s, D_MODEL)

    xs = x_sample.reshape(nb * nt, D_MODEL)
    state_t = jnp.transpose(state_pool[0], (1, 0, 2))
    qs, ks, vs, mas, sgbs, pool_s = _project_sample(xs, state_t, ln1v, w_in16, qn, kn, lin2, scale, w_pa16, nb=nb, nt=nt)
    caches = [_position_minor(c) for c in (cache_kv1, cache_kv2, cache_kv3)]
    attn_s, n1, n2, n3 = _attention_sample(qs, ks, vs, *caches, nb=nb, nt=nt)
    y_sample = _merge_mlp(xs, mas, sgbs, attn_s, w_pb16, w_o16, ln2v, w_up16, w_dn16).reshape(nb, nt, D_MODEL)

    return (y_prompt, y_sample,
            pool_p.reshape(1, b, POOL_STATE, POOL_WIDTH),
            _position_major(kv1), _position_major(kv2), _position_major(kv3),
            jnp.transpose(pool_s, (1, 0, 2))[None],
            _position_major(n1), _position_major(n2), _position_major(n3))
```

```python
import functools

import jax
import jax.numpy as jnp
from jax import lax
from jax.experimental import pallas as pl
from jax.experimental.pallas import tpu as pltpu

F32 = jnp.float32
BF16 = jnp.bfloat16

D_MODEL = 1024
POOL_WINDOWS = (2, 4, 8, 16)
POOL_WIDTH = 512
POOL_GROUP_DIM = 128
POOL_STATE = 15
POOL_HIST = 16
POOL_PAD = 8
ATTN_PATTERNS = ((128, 1), (512, 4), (2048, 16))
BAND = 128
HEAD_DIM = 64
HEAD_SHIFT = 6
HEADS_PER_GROUP = 4
GROUP_WIDTH = HEADS_PER_GROUP * HEAD_DIM
KV_ROWS = 2 * GROUP_WIDTH
QKV_WIDTH = 3 * GROUP_WIDTH
N_PAIRS = QKV_WIDTH // 128
D_FF = 4096
PAST_LEN = 8192
EPS = 1e-6
COL_A, COL_Q, COL_K, COL_V, COL_GA, COL_GB = 0, 512, 1280, 2048, 2816, 3840
Q_SCALE = HEAD_DIM ** -0.5 * 1.4426950408889634
QBLOCK = 128
NEG = -1e30
LANES = 128
VMEM_LIMIT = 56 * 1024 * 1024
VMEM_LIMIT_FUSED = 62 * 1024 * 1024


def _dot(a, b):
    return jnp.dot(a, b, preferred_element_type=F32)


def _dot_nt(a, b):
    return lax.dot_general(a, b, (((1,), (1,)), ((), ())), preferred_element_type=F32)


def _rms_norm_bf16(x, w):
    ms = jnp.mean(x * x, axis=-1, keepdims=True)
    return ((x * lax.rsqrt(ms + EPS)) * w).astype(BF16)


def _sigmoid(x):
    return 1.0 / (1.0 + jnp.exp(-x))


def _segment_mean():
    r = (lax.broadcasted_iota(jnp.int32, (2 * GROUP_WIDTH, GROUP_WIDTH), 0) & (GROUP_WIDTH - 1)) >> HEAD_SHIFT
    c = lax.broadcasted_iota(jnp.int32, (2 * GROUP_WIDTH, GROUP_WIDTH), 1) >> HEAD_SHIFT
    return jnp.where(r == c, 1.0 / HEAD_DIM, 0.0).astype(BF16)


def _head_norm(y, w, seg):
    outs = []
    for g in range(3):
        cols = slice(g * GROUP_WIDTH, (g + 1) * GROUP_WIDTH)
        yg = y[:, cols]
        sq = yg * yg
        hi = sq.astype(BF16)
        lo = (sq - hi.astype(F32)).astype(BF16)
        ms = _dot(jnp.concatenate([hi, lo], axis=1), seg)
        outs.append((yg * lax.rsqrt(ms + EPS)) * w[:, cols])
    return outs


def _project_qkv(u, w_ref, qn_ref, kn_ref, store):
    seg = _segment_mean()
    q = _head_norm(_dot(u, w_ref[:, COL_Q:COL_Q + QKV_WIDTH]), qn_ref[...] * Q_SCALE, seg)
    k = _head_norm(_dot(u, w_ref[:, COL_K:COL_K + QKV_WIDTH]), kn_ref[...], seg)
    v = _dot(u, w_ref[:, COL_V:COL_V + QKV_WIDTH])
    for g in range(3):
        store(g, q[g], k[g], v[:, g * GROUP_WIDTH:(g + 1) * GROUP_WIDTH])


def _pool_branch_and_gates(u, diff, w_ref, lin_ref, scale_ref, wpa_ref, ma_out, sgb_out):
    d16 = diff.astype(BF16)
    z = jnp.concatenate([_dot(d16[:, 0:256], lin_ref[0]), _dot(d16[:, 256:512], lin_ref[1])], axis=1)
    a_mix = (z * scale_ref[...]).astype(BF16)
    branch_a = _dot(a_mix, wpa_ref[...])
    g_a = _dot(u, w_ref[:, COL_GA:COL_GA + D_MODEL])
    ma_out[...] = (_sigmoid(g_a) * branch_a).astype(BF16)
    g_b = _dot(u, w_ref[:, COL_GB:COL_GB + D_MODEL])
    sgb_out[...] = _sigmoid(g_b).astype(BF16)


def _proj_prompt_kernel(x_ref, ln1_ref, w_ref, qn_ref, kn_ref, lin_ref, scale_ref, wpa_ref,
                        q_out, k_out, v_out, ma_out, sgb_out, kv1_out, kv2_out, kv3_out, pool_out,
                        lvl, *, tm):
    t = pl.program_id(1)
    last = pl.num_programs(1) - 1
    u = _rms_norm_bf16(x_ref[0], ln1_ref[...])

    def store(g, q, k, v):
        for hp in range(2):
            cols = slice(hp * LANES, (hp + 1) * LANES)
            q_out[0, 2 * g + hp] = q[:, cols]
            k_out[0, 2 * g + hp] = k[:, cols]
            v_out[0, 2 * g + hp] = v[:, cols]
        if g == 2:
            kv3_out[0, 0:GROUP_WIDTH, :] = k.T
            kv3_out[0, GROUP_WIDTH:, :] = v.T
        else:
            @pl.when(t == last)
            def _():
                if g == 1:
                    kv2_out[0, 0:GROUP_WIDTH, :] = k.T
                    kv2_out[0, GROUP_WIDTH:, :] = v.T
                else:
                    kv1_out[0, 0:GROUP_WIDTH, :] = k[tm - 128:tm].T
                    kv1_out[0, GROUP_WIDTH:, :] = v[tm - 128:tm].T

    _project_qkv(u, w_ref, qn_ref, kn_ref, store)

    nlev = len(POOL_WINDOWS)
    top = POOL_PAD + POOL_HIST

    @pl.when(t == 0)
    def _():
        lvl[:, 0:top, :] = jnp.zeros((nlev, top, POOL_WIDTH), F32)

    lvl[0, top:top + tm, :] = _dot(u, w_ref[:, COL_A:COL_A + POOL_WIDTH])
    pos = t * tm + lax.broadcasted_iota(jnp.int32, (tm, 1), 0)
    diffs = []
    for j, w in enumerate(POOL_WINDOWS):
        assert w == 2 << j and w // 2 <= POOL_PAD
        c0 = j * POOL_GROUP_DIM
        rows = slice(POOL_PAD, top + tm)
        prev = slice(POOL_PAD - w // 2, top + tm - w // 2)
        acc = lvl[j, rows, c0:] + lvl[j, prev, c0:]
        if j + 1 < nlev:
            lvl[j + 1, rows, c0 + POOL_GROUP_DIM:] = acc[:, POOL_GROUP_DIM:]
        cnt = jnp.minimum(pos + 1, w).astype(F32)
        diffs.append(acc[POOL_HIST:, 0:POOL_GROUP_DIM] / cnt - lvl[0, top:top + tm, c0:c0 + POOL_GROUP_DIM])
    diff = jnp.concatenate(diffs, axis=1)

    @pl.when(t == last)
    def _():
        pool_out[0] = lvl[0, top + tm - POOL_STATE:top + tm, :]

    lvl[0, POOL_PAD:top, :] = lvl[0, tm + POOL_PAD:tm + top, :]
    _pool_branch_and_gates(u, diff, w_ref, lin_ref, scale_ref, wpa_ref, ma_out.at[0], sgb_out.at[0])


def _const_spec(shape):
    nd = len(shape)
    return pl.BlockSpec(shape, lambda *_: (0,) * nd, pipeline_mode=pl.Buffered(1))


def _project_prompt(x, ln1, w_in, qn, kn, lin2, scale, w_pa, *, tm=512):
    b, s, _ = x.shape
    nt = s // tm
    assert s % tm == 0 and tm == ATTN_PATTERNS[1][0] and s == ATTN_PATTERNS[2][0]
    row = lambda width: pl.BlockSpec((1, tm, width), lambda i, j: (i, j, 0))
    tail = lambda rows, width: pl.BlockSpec((1, rows, width), lambda i, j: (i, 0, 0))
    pairs = pl.BlockSpec((1, N_PAIRS, tm, LANES), lambda i, j: (i, 0, j, 0))
    out_shape = (
        jax.ShapeDtypeStruct((b, N_PAIRS, s, LANES), F32),
        jax.ShapeDtypeStruct((b, N_PAIRS, s, LANES), F32),
        jax.ShapeDtypeStruct((b, N_PAIRS, s, LANES), F32),
        jax.ShapeDtypeStruct((b, s, D_MODEL), BF16),
        jax.ShapeDtypeStruct((b, s, D_MODEL), BF16),
        jax.ShapeDtypeStruct((b, KV_ROWS, 128), F32),
        jax.ShapeDtypeStruct((b, KV_ROWS, 512), F32),
        jax.ShapeDtypeStruct((b, KV_ROWS, s), F32),
        jax.ShapeDtypeStruct((b, POOL_STATE, POOL_WIDTH), F32),
    )
    return pl.pallas_call(
        functools.partial(_proj_prompt_kernel, tm=tm),
        grid=(b, nt),
        in_specs=[row(D_MODEL), _const_spec(ln1.shape), _const_spec(w_in.shape), _const_spec(qn.shape),
                  _const_spec(kn.shape), _const_spec(lin2.shape), _const_spec(scale.shape),
                  _const_spec(w_pa.shape)],
        out_specs=(pairs, pairs, pairs, row(D_MODEL), row(D_MODEL),
                   tail(KV_ROWS, 128), tail(KV_ROWS, 512),
                   pl.BlockSpec((1, KV_ROWS, tm), lambda i, j: (i, 0, j)),
                   tail(POOL_STATE, POOL_WIDTH)),
        out_shape=out_shape,
        scratch_shapes=[pltpu.VMEM((len(POOL_WINDOWS), POOL_PAD + POOL_HIST + tm, POOL_WIDTH), F32)],
        compiler_params=pltpu.CompilerParams(dimension_semantics=("arbitrary", "arbitrary"),
                                             vmem_limit_bytes=VMEM_LIMIT),
        name="project_prompt",
    )(x, ln1, w_in, qn, kn, lin2, scale, w_pa)


def _proj_sample_kernel(x_ref, st_ref, ln1_ref, w_ref, qn_ref, kn_ref, lin_ref, scale_ref, wpa_ref,
                        q_out, k_out, v_out, ma_out, sgb_out, pool_out, abuf, dbuf, *, nb, nt):
    u = _rms_norm_bf16(x_ref[...], ln1_ref[...])

    def store(g, q, k, v):
        cols = slice(g * GROUP_WIDTH, (g + 1) * GROUP_WIDTH)
        q_out[:, cols] = q
        k_out[:, cols] = k
        v_out[:, cols] = v

    _project_qkv(u, w_ref, qn_ref, kn_ref, store)

    a = _dot(u, w_ref[:, COL_A:COL_A + POOL_WIDTH])
    for g in range(len(POOL_WINDOWS)):
        abuf[g] = a[:, g * POOL_GROUP_DIM:(g + 1) * POOL_GROUP_DIM]

    def slab(j, g):
        if j < POOL_STATE:
            return st_ref[j, :, g * POOL_GROUP_DIM:(g + 1) * POOL_GROUP_DIM]
        return abuf[g, pl.ds(j - POOL_STATE, nb, stride=nt), :]

    for t in range(nt):
        for g, w in enumerate(POOL_WINDOWS):
            a_t = slab(POOL_STATE + t, g)
            acc = a_t
            for s in range(1, w):
                acc = acc + slab(POOL_STATE + t - s, g)
            cnt = float(min(PAST_LEN + t + 1, w))
            dbuf[g, pl.ds(t, nb, stride=nt), :] = acc / cnt - a_t

    for j in range(POOL_STATE):
        pool_out[j] = jnp.concatenate([slab(j + nt, g) for g in range(len(POOL_WINDOWS))], axis=1)

    diff = jnp.concatenate([dbuf[g] for g in range(len(POOL_WINDOWS))], axis=1)
    _pool_branch_and_gates(u, diff, w_ref, lin_ref, scale_ref, wpa_ref, ma_out, sgb_out)


def _project_sample(x2d, state_t, ln1, w_in, qn, kn, lin2, scale, w_pa, *, nb, nt):
    m = nb * nt
    assert nt <= POOL_STATE and x2d.shape == (m, D_MODEL) and state_t.shape == (POOL_STATE, nb, POOL_WIDTH)
    args = (x2d, state_t, ln1, w_in, qn, kn, lin2, scale, w_pa)
    out_shape = (
        jax.ShapeDtypeStruct((m, QKV_WIDTH), F32),
        jax.ShapeDtypeStruct((m, QKV_WIDTH), F32),
        jax.ShapeDtypeStruct((m, QKV_WIDTH), F32),
        jax.ShapeDtypeStruct((m, D_MODEL), BF16),
        jax.ShapeDtypeStruct((m, D_MODEL), BF16),
        jax.ShapeDtypeStruct((POOL_STATE, nb, POOL_WIDTH), F32),
    )
    group_buf = pltpu.VMEM((len(POOL_WINDOWS), m, POOL_GROUP_DIM), F32)
    return pl.pallas_call(
        functools.partial(_proj_sample_kernel, nb=nb, nt=nt),
        grid=(1,),
        in_specs=[_const_spec(a.shape) for a in args],
        out_specs=tuple(pl.BlockSpec(o.shape, lambda i, nd=len(o.shape): (0,) * nd) for o in out_shape),
        out_shape=out_shape,
        scratch_shapes=[group_buf, group_buf],
        compiler_params=pltpu.CompilerParams(dimension_semantics=("arbitrary",), vmem_limit_bytes=VMEM_LIMIT),
        name="project_sample",
    )(*args)


def _rows(start, n, stride):
    return pl.ds(start, n) if stride == 1 else pl.ds(start, n, stride=stride)


def _band_bias(nk, offset):
    qi = lax.broadcasted_iota(jnp.int32, (2 * QBLOCK, nk), 0) & (QBLOCK - 1)
    kj = lax.broadcasted_iota(jnp.int32, (2 * QBLOCK, nk), 1)
    dist = qi + offset - kj
    return jnp.where((dist >= 0) & (dist <= BAND), 0.0, NEG).astype(F32)


def _attn_prompt_kernel(q_ref, k_ref, v_ref, o_ref, o1_buf, l1_buf, o2_buf, l2_buf, s_buf, bias_buf, *, seq):
    step = pl.program_id(1)
    lane = lax.broadcasted_iota(jnp.int32, (QBLOCK, LANES), 1)
    first = lane < HEAD_DIM
    bias_buf[0] = _band_bias(2 * QBLOCK, 0)
    bias_buf[1] = _band_bias(2 * QBLOCK, QBLOCK)

    def scores(slot, hp, qstart, kstart, nk, dil, bias_sel):
        qrows, krows = _rows(qstart, QBLOCK, dil), _rows(kstart, nk, dil)
        q16 = q_ref[0, hp, qrows, :].astype(BF16)
        zero = jnp.zeros_like(q16)
        q2 = jnp.concatenate([jnp.where(first, q16, zero), jnp.where(first, zero, q16)], axis=0)
        s = _dot_nt(q2, k_ref[0, hp, krows, :].astype(BF16))
        s_buf[slot, hp, :, 0:nk] = s + bias_buf[bias_sel, :, 0:nk]

    def values(slot, hp, qstart, kstart, nk, dil, emit):
        qrows, krows = _rows(qstart, QBLOCK, dil), _rows(kstart, nk, dil)
        s = s_buf[slot, hp, :, 0:nk]
        m = jnp.max(s, axis=-1, keepdims=True)
        p = jnp.exp2(s - m).astype(BF16)
        v_ext = jnp.concatenate([v_ref[0, hp, krows, :].astype(BF16), jnp.ones((nk, LANES), BF16)], axis=1)
        o = _dot(p, v_ext)
        num = jnp.where(first, o[:QBLOCK, :LANES], o[QBLOCK:, :LANES])
        den = jnp.where(first, o[:QBLOCK, LANES:], o[QBLOCK:, LANES:])
        lse = jnp.where(first, m[:QBLOCK], m[QBLOCK:]) + jnp.log2(den)
        emit(qrows, hp, num / den, lse)

    def run(ntasks, task, nk, dil, emit):
        assert ntasks % 4 == 0

        def stage(score_slot, i_score, value_slot, i_value):
            for d in range(2):
                sq, sk, sel = task(jnp.minimum(i_score + d, ntasks - 1))
                vq, vk, _ = task(i_value + d)
                for hp in range(2):
                    scores(score_slot + d, hp, sq, sk, nk, dil, sel)
                    values(value_slot + d, hp, vq, vk, nk, dil, emit)

        for d in range(2):
            for hp in range(2):
                scores(d, hp, *task(d)[:2], nk, dil, task(d)[2])

        def body(j, c):
            i0 = 4 * j
            stage(2, i0 + 2, 0, i0)
            stage(0, i0 + 4, 2, i0 + 2)
            return c
        lax.fori_loop(0, ntasks // 4, body, 0)

    def store_to(o_buf, l_buf):
        def emit(qrows, hp, out, lse):
            o_buf[hp, qrows, :] = out
            l_buf[hp, qrows, :] = lse
        return emit

    @pl.when(step == 0)
    def _():
        dil = 16
        assert seq == dil * QBLOCK
        run(dil, lambda i: (i, i, 0), QBLOCK, dil, store_to(o2_buf, l2_buf))

    @pl.when(step == 1)
    def _():
        dil = 4
        nblk = seq // (dil * QBLOCK)
        assert nblk == 4

        def task(i):
            r, nb = i >> 2, i & 3
            prev = jnp.minimum(nb, 1)
            return r + nb * (dil * QBLOCK), r + (nb - prev) * (dil * QBLOCK), prev
        run(dil * nblk, task, 2 * QBLOCK, dil, store_to(o1_buf, l1_buf))

    @pl.when(step == 2)
    def _():
        def emit(qrows, hp, out0, lse0):
            lse1, lse2 = l1_buf[hp, qrows, :], l2_buf[hp, qrows, :]
            m = jnp.maximum(jnp.maximum(lse0, lse1), lse2)
            w0, w1, w2 = jnp.exp2(lse0 - m), jnp.exp2(lse1 - m), jnp.exp2(lse2 - m)
            num = w0 * out0 + w1 * o1_buf[hp, qrows, :] + w2 * o2_buf[hp, qrows, :]
            o_ref[0, qrows, hp * LANES:(hp + 1) * LANES] = (num / (w0 + w1 + w2)).astype(o_ref.dtype)

        def task(i):
            prev = jnp.minimum(i, 1)
            return pl.multiple_of(jnp.int32(i) * QBLOCK, QBLOCK), pl.multiple_of((i - prev) * QBLOCK, QBLOCK), prev
        run(seq // QBLOCK, task, 2 * QBLOCK, 1, emit)


def _attention_prompt(q, k, v):
    b, _, s, _ = q.shape
    spec = pl.BlockSpec((1, 2, s, LANES), lambda i, j: (i, 2 - j, 0, 0))
    group_buf = pltpu.VMEM((2, s, LANES), F32)
    return pl.pallas_call(
        functools.partial(_attn_prompt_kernel, seq=s),
        grid=(b, 3),
        in_specs=[spec, spec, spec],
        out_specs=pl.BlockSpec((1, s, GROUP_WIDTH), lambda i, j: (i, 0, 0)),
        out_shape=jax.ShapeDtypeStruct((b, s, GROUP_WIDTH), BF16),
        scratch_shapes=[group_buf, group_buf, group_buf, group_buf,
                        pltpu.VMEM((4, 2, 2 * QBLOCK, 2 * QBLOCK), F32),
                        pltpu.VMEM((2, 2 * QBLOCK, 2 * QBLOCK), F32)],
        compiler_params=pltpu.CompilerParams(dimension_semantics=("arbitrary", "arbitrary"),
                                             vmem_limit_bytes=VMEM_LIMIT),
        name="attention_prompt",
    )(q, k, v)


def _attn_sample_parts(q_ref, k_ref, v_ref, c1_ref, c2_ref, c3_ref, o_ref, n1_ref, n2_ref, n3_ref, *, nt):
    nrow = HEADS_PER_GROUP * nt
    nt_shift = nt.bit_length() - 1
    assert nt == 1 << nt_shift
    row_head = lax.broadcasted_iota(jnp.int32, (nrow, GROUP_WIDTH), 0) >> nt_shift
    col_head = lax.broadcasted_iota(jnp.int32, (nrow, GROUP_WIDTH), 1) >> HEAD_SHIFT
    own = row_head == col_head
    col_head_t = lax.broadcasted_iota(jnp.int32, (nt, GROUP_WIDTH), 1) >> HEAD_SHIFT
    tail_lane = lax.broadcasted_iota(jnp.int32, (KV_ROWS, LANES), 1) >= LANES - nt

    def own_blocks(x):
        acc = jnp.zeros((nt, GROUP_WIDTH), F32)
        for h in range(HEADS_PER_GROUP):
            acc = acc + jnp.where(col_head_t == h, x[h * nt:(h + 1) * nt], 0.0)
        return acc

    def group(g):
        c_ref, n_ref = ((c1_ref, n1_ref), (c2_ref, n2_ref), (c3_ref, n3_ref))[g]
        win, dil = ATTN_PATTERNS[g]
        assert c_ref.shape == (1, KV_ROWS, win) and dil & (dil - 1) == 0
        cs = g * GROUP_WIDTH
        k_new = k_ref[:, cs:cs + GROUP_WIDTH]
        v_new = v_ref[:, cs:cs + GROUP_WIDTH]

        new_t = jnp.concatenate([jnp.zeros((LANES - nt, KV_ROWS), F32), jnp.concatenate([k_new, v_new], axis=1)],
                                axis=0).T
        ntiles = win // LANES
        rot = pltpu.roll(c_ref[0, :, 0:LANES], LANES - nt, axis=1)
        for j in range(ntiles):
            nxt = pltpu.roll(c_ref[0, :, (j + 1) * LANES:(j + 2) * LANES], LANES - nt, axis=1) if j + 1 < ntiles else new_t
            n_ref[0, :, j * LANES:(j + 1) * LANES] = jnp.where(tail_lane, nxt, rot)
            rot = nxt

        q = q_ref[:, cs:cs + GROUP_WIDTH]
        qbd = jnp.where(own, jnp.concatenate([q] * HEADS_PER_GROUP, axis=0), 0.0).astype(BF16)
        k_old = c_ref[0, 0:GROUP_WIDTH, :].astype(BF16)
        v_old = c_ref[0, GROUP_WIDTH:, :].astype(BF16)
        t_old = lax.broadcasted_iota(jnp.int32, (nrow, win), 0) & (nt - 1)
        i_old = lax.broadcasted_iota(jnp.int32, (nrow, win), 1)
        ok_old = (i_old >= t_old) & (((i_old - t_old) & (dil - 1)) == 0)
        t_new = lax.broadcasted_iota(jnp.int32, (nrow, nt), 0) & (nt - 1)
        i_new = lax.broadcasted_iota(jnp.int32, (nrow, nt), 1)
        ok_new = (i_new <= t_new) & (((t_new - i_new) & (dil - 1)) == 0)
        s_old = _dot(qbd, k_old) + jnp.where(ok_old, 0.0, NEG)
        s_new = _dot_nt(qbd, k_new.astype(BF16)) + jnp.where(ok_new, 0.0, NEG)
        m = jnp.maximum(jnp.max(s_old, axis=-1, keepdims=True), jnp.max(s_new, axis=-1, keepdims=True))
        p_old = jnp.exp2(s_old - m)
        p_new = jnp.exp2(s_new - m)
        den = jnp.sum(p_old, axis=-1, keepdims=True) + jnp.sum(p_new, axis=-1, keepdims=True)
        o = (_dot_nt(p_old.astype(BF16), v_old) + _dot(p_new.astype(BF16), v_new.astype(BF16))) / den
        return own_blocks(o), own_blocks(jnp.broadcast_to(m + jnp.log2(den), (nrow, GROUP_WIDTH)))

    def combine(parts):
        (o0, l0), (o1, l1), (o2, l2) = parts
        m = jnp.maximum(jnp.maximum(l0, l1), l2)
        w0, w1, w2 = jnp.exp2(l0 - m), jnp.exp2(l1 - m), jnp.exp2(l2 - m)
        o_ref[...] = (w0 * o0 + w1 * o1 + w2 * o2) / (w0 + w1 + w2)

    return group, combine


N_MERGE_IN = 9
N_SAMPLE_IN = 6


def _merge_mlp_kernel(*refs, ff_chunk, nt_sample):
    x_ref, ma_ref, sgb_ref, at_ref, wpb_ref, wo_ref, ln2_ref, wup_ref, wdn_ref = refs[:N_MERGE_IN]
    y_ref = refs[N_MERGE_IN + (N_SAMPLE_IN if nt_sample else 0)]
    pending = []
    if nt_sample:
        group, combine = _attn_sample_parts(*refs[N_MERGE_IN:N_MERGE_IN + N_SAMPLE_IN],
                                            *refs[N_MERGE_IN + N_SAMPLE_IN + 1:], nt=nt_sample)
        parts = [None] * len(ATTN_PATTERNS)

        def run_group(g):
            parts[g] = group(g)

        def run_combine():
            combine(parts)
        pending = [functools.partial(run_group, g) for g in (2, 1, 0)] + [run_combine]

    def issue_sample_piece():
        if pending:
            pending.pop(0)()

    issue_sample_piece()
    branch_b = _dot(at_ref[...].astype(BF16), wpb_ref[...])
    mixed = (ma_ref[...].astype(F32) + sgb_ref[...].astype(F32) * branch_b).astype(BF16)
    h = x_ref[...] + _dot(mixed, wo_ref[...])
    u = _rms_norm_bf16(h, ln2_ref[...])
    acc = h
    for c in range(0, D_FF, ff_chunk):
        z = jnp.maximum(_dot(u, wup_ref[:, c:c + ff_chunk]), 0.0)
        acc = acc + _dot((z * z).astype(BF16), wdn_ref[c:c + ff_chunk, :])
        issue_sample_piece()
    y_ref[...] = acc
    while pending:
        issue_sample_piece()


def _merge_mlp(x2d, ma, sgb, attn, w_pb, w_o, ln2, w_up, w_down, sample=None, *, tm=512, ff_chunk=1024):
    m = x2d.shape[0]
    tm = min(tm, m)
    assert m % tm == 0
    row = lambda width: pl.BlockSpec((tm, width), lambda i: (i, 0))
    args = [x2d, ma, sgb, attn, w_pb, w_o, ln2, w_up, w_down]
    in_specs = [row(D_MODEL), row(D_MODEL), row(D_MODEL), row(GROUP_WIDTH)] + [_const_spec(w.shape) for w in args[4:]]
    out_specs = [row(D_MODEL)]
    out_shape = [jax.ShapeDtypeStruct((m, D_MODEL), F32)]
    nt = 0
    if sample is not None:
        *sample_args, nt = sample
        q, c1, c2, c3 = sample_args[0], *sample_args[3:]
        nb = c1.shape[0]
        assert m // tm == nb and q.shape == (nb * nt, QKV_WIDTH), "one sample batch element per token tile"
        tok = pl.BlockSpec((nt, QKV_WIDTH), lambda i: (i, 0))
        cache = lambda c: pl.BlockSpec((1,) + c.shape[1:], lambda i: (i, 0, 0))
        args += sample_args
        in_specs += [tok, tok, tok, cache(c1), cache(c2), cache(c3)]
        out_specs += [pl.BlockSpec((nt, GROUP_WIDTH), lambda i: (i, 0)), cache(c1), cache(c2), cache(c3)]
        out_shape += [jax.ShapeDtypeStruct((nb * nt, GROUP_WIDTH), F32)] + [jax.ShapeDtypeStruct(c.shape, F32) for c in (c1, c2, c3)]
    out = pl.pallas_call(
        functools.partial(_merge_mlp_kernel, ff_chunk=ff_chunk, nt_sample=nt),
        grid=(m // tm,),
        in_specs=in_specs,
        out_specs=tuple(out_specs),
        out_shape=tuple(out_shape),
        compiler_params=pltpu.CompilerParams(dimension_semantics=("arbitrary",),
                                             vmem_limit_bytes=VMEM_LIMIT_FUSED if sample is not None else VMEM_LIMIT),
        name="merge_mlp",
    )(*args)
    return out if sample is not None else out[0]


def _position_minor(c):
    n, w = c.shape[1], c.shape[2]
    return jnp.transpose(c[0], (0, 2, 3, 4, 1)).reshape(n, KV_ROWS, w)


def _position_major(c):
    n, _, w = c.shape
    return jnp.transpose(c.reshape(n, 2, HEADS_PER_GROUP, HEAD_DIM, w), (0, 4, 1, 2, 3))[None]


def kernel(x_prompt, x_sample, state_pool, cache_kv1, cache_kv2, cache_kv3, ln1, w_in, q_norm, k_norm,
           pool_lin, pool_scale, w_pa, w_pb, w_o, ln2, w_up, w_down):
    depth = ln1.shape[0]
    assert depth == 1
    b, s, _ = x_prompt.shape
    nb, nt, _ = x_sample.shape

    w_in16, w_pa16, w_pb16 = w_in[0].astype(BF16), w_pa[0].astype(BF16), w_pb[0].astype(BF16)
    w_o16, w_up16, w_dn16 = w_o[0].astype(BF16), w_up[0].astype(BF16), w_down[0].astype(BF16)
    ln1v, ln2v = ln1[0].reshape(1, D_MODEL), ln2[0].reshape(1, D_MODEL)
    qn, kn = q_norm[0].reshape(1, QKV_WIDTH), k_norm[0].reshape(1, QKV_WIDTH)
    scale = pool_scale[0].reshape(1, POOL_WIDTH)
    lin = pool_lin[0].astype(BF16)
    zeros = jnp.zeros((POOL_GROUP_DIM, POOL_GROUP_DIM), BF16)
    lin2 = jnp.stack([jnp.block([[lin[0], zeros], [zeros, lin[1]]]), jnp.block([[lin[2], zeros], [zeros, lin[3]]])])

    q, k, v, ma, sgb, kv1, kv2, kv3, pool_p = _project_prompt(x_prompt, ln1v, w_in16, qn, kn, lin2, scale, w_pa16)
    xs = x_sample.reshape(nb * nt, D_MODEL)
    state_t = jnp.transpose(state_pool[0], (1, 0, 2))
    qs, ks, vs, mas, sgbs, pool_s = _project_sample(xs, state_t, ln1v, w_in16, qn, kn, lin2, scale, w_pa16, nb=nb, nt=nt)

    attn = _attention_prompt(q, k, v)
    caches = [_position_minor(c) for c in (cache_kv1, cache_kv2, cache_kv3)]
    y_prompt, attn_s, n1, n2, n3 = _merge_mlp(
        x_prompt.reshape(b * s, D_MODEL), ma.reshape(b * s, D_MODEL), sgb.reshape(b * s, D_MODEL),
        attn.reshape(b * s, GROUP_WIDTH), w_pb16, w_o16, ln2v, w_up16, w_dn16, sample=(qs, ks, vs, *caches, nt))
    y_prompt = y_prompt.reshape(b, s, D_MODEL)
    y_sample = _merge_mlp(xs, mas, sgbs, attn_s, w_pb16, w_o16, ln2v, w_up16, w_dn16).reshape(nb, nt, D_MODEL)

    return (y_prompt, y_sample,
            pool_p.reshape(1, b, POOL_STATE, POOL_WIDTH),
            _position_major(kv1), _position_major(kv2), _position_major(kv3),
            jnp.transpose(pool_s, (1, 0, 2))[None],
            _position_major(n1), _position_major(n2), _position_major(n3))
```

```python
import functools

import jax
import jax.numpy as jnp
from jax import lax
from jax.experimental import pallas as pl
from jax.experimental.pallas import tpu as pltpu

F32 = jnp.float32
BF16 = jnp.bfloat16

D_MODEL = 1024
POOL_WINDOWS = (2, 4, 8, 16)
POOL_WIDTH = 512
POOL_GROUP_DIM = 128
POOL_STATE = 15
POOL_HIST = 16
POOL_PAD = 8
ATTN_PATTERNS = ((128, 1), (512, 4), (2048, 16))
BAND = 128
HEAD_DIM = 64
HEAD_SHIFT = 6
HEADS_PER_GROUP = 4
GROUP_WIDTH = HEADS_PER_GROUP * HEAD_DIM
KV_ROWS = 2 * GROUP_WIDTH
QKV_WIDTH = 3 * GROUP_WIDTH
N_PAIRS = QKV_WIDTH // 128
D_FF = 4096
PAST_LEN = 8192
EPS = 1e-6
COL_A, COL_Q, COL_K, COL_V, COL_GA, COL_GB = 0, 512, 1280, 2048, 2816, 3840
Q_SCALE = HEAD_DIM ** -0.5 * 1.4426950408889634
QBLOCK = 128
NEG = -1e30
LANES = 128
VMEM_LIMIT = 56 * 1024 * 1024
VMEM_LIMIT_FUSED = 62 * 1024 * 1024


def _dot(a, b):
    return jnp.dot(a, b, preferred_element_type=F32)


def _dot_nt(a, b):
    return lax.dot_general(a, b, (((1,), (1,)), ((), ())), preferred_element_type=F32)


def _rms_norm_bf16(x, w):
    ms = jnp.mean(x * x, axis=-1, keepdims=True)
    return ((x * lax.rsqrt(ms + EPS)) * w).astype(BF16)


def _sigmoid(x):
    return 1.0 / (1.0 + jnp.exp(-x))


def _segment_mean():
    r = (lax.broadcasted_iota(jnp.int32, (2 * GROUP_WIDTH, GROUP_WIDTH), 0) & (GROUP_WIDTH - 1)) >> HEAD_SHIFT
    c = lax.broadcasted_iota(jnp.int32, (2 * GROUP_WIDTH, GROUP_WIDTH), 1) >> HEAD_SHIFT
    return jnp.where(r == c, 1.0 / HEAD_DIM, 0.0).astype(BF16)


def _head_norm(y, w, seg):
    outs = []
    for g in range(3):
        cols = slice(g * GROUP_WIDTH, (g + 1) * GROUP_WIDTH)
        yg = y[:, cols]
        sq = yg * yg
        hi = sq.astype(BF16)
        lo = (sq - hi.astype(F32)).astype(BF16)
        ms = _dot(jnp.concatenate([hi, lo], axis=1), seg)
        outs.append((yg * lax.rsqrt(ms + EPS)) * w[:, cols])
    return outs


def _project_qkv(u, w_ref, qn_ref, kn_ref, store):
    seg = _segment_mean()
    q = _head_norm(_dot(u, w_ref[:, COL_Q:COL_Q + QKV_WIDTH]), qn_ref[...] * Q_SCALE, seg)
    k = _head_norm(_dot(u, w_ref[:, COL_K:COL_K + QKV_WIDTH]), kn_ref[...], seg)
    v = _dot(u, w_ref[:, COL_V:COL_V + QKV_WIDTH])
    for g in range(3):
        store(g, q[g], k[g], v[:, g * GROUP_WIDTH:(g + 1) * GROUP_WIDTH])


def _pool_branch_and_gates(u, diff, w_ref, lin_ref, scale_ref, wpa_ref, ma_out, sgb_out):
    d16 = diff.astype(BF16)
    z = jnp.concatenate([_dot(d16[:, 0:256], lin_ref[0]), _dot(d16[:, 256:512], lin_ref[1])], axis=1)
    a_mix = (z * scale_ref[...]).astype(BF16)
    branch_a = _dot(a_mix, wpa_ref[...])
    g_a = _dot(u, w_ref[:, COL_GA:COL_GA + D_MODEL])
    ma_out[...] = (_sigmoid(g_a) * branch_a).astype(BF16)
    g_b = _dot(u, w_ref[:, COL_GB:COL_GB + D_MODEL])
    sgb_out[...] = _sigmoid(g_b).astype(BF16)


def _proj_prompt_kernel(x_ref, ln1_ref, w_ref, qn_ref, kn_ref, lin_ref, scale_ref, wpa_ref,
                        q_out, k_out, v_out, ma_out, sgb_out, kv1_out, kv2_out, kv3_out, pool_out,
                        lvl, *, tm):
    t = pl.program_id(1)
    last = pl.num_programs(1) - 1
    nlev = len(POOL_WINDOWS)
    top = POOL_PAD + POOL_HIST

    @pl.when(t == 0)
    def _():
        lvl[:, 0:top, :] = jnp.zeros((nlev, top, POOL_WIDTH), F32)

    u = _rms_norm_bf16(x_ref[0], ln1_ref[...])

    def store(g, q, k, v):
        for hp in range(2):
            cols = slice(hp * LANES, (hp + 1) * LANES)
            q_out[0, 2 * g + hp] = q[:, cols]
            k_out[0, 2 * g + hp] = k[:, cols]
            v_out[0, 2 * g + hp] = v[:, cols]
        if g == 2:
            kv3_out[0, 0:GROUP_WIDTH, :] = k.T
            kv3_out[0, GROUP_WIDTH:, :] = v.T

    _project_qkv(u, w_ref, qn_ref, kn_ref, store)

    lvl[0, top:top + tm, :] = _dot(u, w_ref[:, COL_A:COL_A + POOL_WIDTH])
    pos = t * tm + lax.broadcasted_iota(jnp.int32, (tm, 1), 0)
    diffs = []
    for j, w in enumerate(POOL_WINDOWS):
        assert w == 2 << j and w // 2 <= POOL_PAD
        c0 = j * POOL_GROUP_DIM
        rows = slice(POOL_PAD, top + tm)
        prev = slice(POOL_PAD - w // 2, top + tm - w // 2)
        acc = lvl[j, rows, c0:] + lvl[j, prev, c0:]
        if j + 1 < nlev:
            lvl[j + 1, rows, c0 + POOL_GROUP_DIM:] = acc[:, POOL_GROUP_DIM:]
        cnt = jnp.minimum(pos + 1, w).astype(F32)
        diffs.append(acc[POOL_HIST:, 0:POOL_GROUP_DIM] / cnt - lvl[0, top:top + tm, c0:c0 + POOL_GROUP_DIM])
    diff = jnp.concatenate(diffs, axis=1)

    _pool_branch_and_gates(u, diff, w_ref, lin_ref, scale_ref, wpa_ref, ma_out.at[0], sgb_out.at[0])

    @pl.when(t == last)
    def _():
        def pair_rows(ref, g, r0):
            return jnp.concatenate([ref[0, 2 * g, r0:tm, :], ref[0, 2 * g + 1, r0:tm, :]], axis=1)
        kv2_out[0, 0:GROUP_WIDTH, :] = pair_rows(k_out, 1, 0).T
        kv2_out[0, GROUP_WIDTH:, :] = pair_rows(v_out, 1, 0).T
        kv1_out[0, 0:GROUP_WIDTH, :] = pair_rows(k_out, 0, tm - 128).T
        kv1_out[0, GROUP_WIDTH:, :] = pair_rows(v_out, 0, tm - 128).T
        pool_out[0] = lvl[0, top + tm - POOL_STATE:top + tm, :]

    lvl[0, POOL_PAD:top, :] = lvl[0, tm + POOL_PAD:tm + top, :]


def _const_spec(shape):
    nd = len(shape)
    return pl.BlockSpec(shape, lambda *_: (0,) * nd, pipeline_mode=pl.Buffered(1))


def _project_prompt(x, ln1, w_in, qn, kn, lin2, scale, w_pa, *, tm=512):
    b, s, _ = x.shape
    nt = s // tm
    assert s % tm == 0 and tm == ATTN_PATTERNS[1][0] and s == ATTN_PATTERNS[2][0]
    row = lambda width: pl.BlockSpec((1, tm, width), lambda i, j: (i, j, 0))
    tail = lambda rows, width: pl.BlockSpec((1, rows, width), lambda i, j: (i, 0, 0))
    pairs = pl.BlockSpec((1, N_PAIRS, tm, LANES), lambda i, j: (i, 0, j, 0))
    out_shape = (
        jax.ShapeDtypeStruct((b, N_PAIRS, s, LANES), F32),
        jax.ShapeDtypeStruct((b, N_PAIRS, s, LANES), F32),
        jax.ShapeDtypeStruct((b, N_PAIRS, s, LANES), F32),
        jax.ShapeDtypeStruct((b, s, D_MODEL), BF16),
        jax.ShapeDtypeStruct((b, s, D_MODEL), BF16),
        jax.ShapeDtypeStruct((b, KV_ROWS, 128), F32),
        jax.ShapeDtypeStruct((b, KV_ROWS, 512), F32),
        jax.ShapeDtypeStruct((b, KV_ROWS, s), F32),
        jax.ShapeDtypeStruct((b, POOL_STATE, POOL_WIDTH), F32),
    )
    return pl.pallas_call(
        functools.partial(_proj_prompt_kernel, tm=tm),
        grid=(b, nt),
        in_specs=[row(D_MODEL), _const_spec(ln1.shape), _const_spec(w_in.shape), _const_spec(qn.shape),
                  _const_spec(kn.shape), _const_spec(lin2.shape), _const_spec(scale.shape),
                  _const_spec(w_pa.shape)],
        out_specs=(pairs, pairs, pairs, row(D_MODEL), row(D_MODEL),
                   tail(KV_ROWS, 128), tail(KV_ROWS, 512),
                   pl.BlockSpec((1, KV_ROWS, tm), lambda i, j: (i, 0, j)),
                   tail(POOL_STATE, POOL_WIDTH)),
        out_shape=out_shape,
        scratch_shapes=[pltpu.VMEM((len(POOL_WINDOWS), POOL_PAD + POOL_HIST + tm, POOL_WIDTH), F32)],
        compiler_params=pltpu.CompilerParams(dimension_semantics=("arbitrary", "arbitrary"),
                                             vmem_limit_bytes=VMEM_LIMIT),
        name="project_prompt",
    )(x, ln1, w_in, qn, kn, lin2, scale, w_pa)


def _proj_sample_kernel(x_ref, st_ref, ln1_ref, w_ref, qn_ref, kn_ref, lin_ref, scale_ref, wpa_ref,
                        q_out, k_out, v_out, ma_out, sgb_out, pool_out, abuf, dbuf, *, nb, nt):
    u = _rms_norm_bf16(x_ref[...], ln1_ref[...])

    def store(g, q, k, v):
        cols = slice(g * GROUP_WIDTH, (g + 1) * GROUP_WIDTH)
        q_out[:, cols] = q
        k_out[:, cols] = k
        v_out[:, cols] = v

    _project_qkv(u, w_ref, qn_ref, kn_ref, store)

    a = _dot(u, w_ref[:, COL_A:COL_A + POOL_WIDTH])
    for g in range(len(POOL_WINDOWS)):
        abuf[g] = a[:, g * POOL_GROUP_DIM:(g + 1) * POOL_GROUP_DIM]

    def slab(j, g):
        if j < POOL_STATE:
            return st_ref[j, :, g * POOL_GROUP_DIM:(g + 1) * POOL_GROUP_DIM]
        return abuf[g, pl.ds(j - POOL_STATE, nb, stride=nt), :]

    for t in range(nt):
        for g, w in enumerate(POOL_WINDOWS):
            a_t = slab(POOL_STATE + t, g)
            acc = a_t
            for s in range(1, w):
                acc = acc + slab(POOL_STATE + t - s, g)
            cnt = float(min(PAST_LEN + t + 1, w))
            dbuf[g, pl.ds(t, nb, stride=nt), :] = acc / cnt - a_t

    for j in range(POOL_STATE):
        pool_out[j] = jnp.concatenate([slab(j + nt, g) for g in range(len(POOL_WINDOWS))], axis=1)

    diff = jnp.concatenate([dbuf[g] for g in range(len(POOL_WINDOWS))], axis=1)
    _pool_branch_and_gates(u, diff, w_ref, lin_ref, scale_ref, wpa_ref, ma_out, sgb_out)


def _project_sample(x2d, state_t, ln1, w_in, qn, kn, lin2, scale, w_pa, *, nb, nt):
    m = nb * nt
    assert nt <= POOL_STATE and x2d.shape == (m, D_MODEL) and state_t.shape == (POOL_STATE, nb, POOL_WIDTH)
    args = (x2d, state_t, ln1, w_in, qn, kn, lin2, scale, w_pa)
    out_shape = (
        jax.ShapeDtypeStruct((m, QKV_WIDTH), F32),
        jax.ShapeDtypeStruct((m, QKV_WIDTH), F32),
        jax.ShapeDtypeStruct((m, QKV_WIDTH), F32),
        jax.ShapeDtypeStruct((m, D_MODEL), BF16),
        jax.ShapeDtypeStruct((m, D_MODEL), BF16),
        jax.ShapeDtypeStruct((POOL_STATE, nb, POOL_WIDTH), F32),
    )
    group_buf = pltpu.VMEM((len(POOL_WINDOWS), m, POOL_GROUP_DIM), F32)
    return pl.pallas_call(
        functools.partial(_proj_sample_kernel, nb=nb, nt=nt),
        grid=(1,),
        in_specs=[_const_spec(a.shape) for a in args],
        out_specs=tuple(pl.BlockSpec(o.shape, lambda i, nd=len(o.shape): (0,) * nd) for o in out_shape),
        out_shape=out_shape,
        scratch_shapes=[group_buf, group_buf],
        compiler_params=pltpu.CompilerParams(dimension_semantics=("arbitrary",), vmem_limit_bytes=VMEM_LIMIT),
        name="project_sample",
    )(*args)


def _rows(start, n, stride):
    return pl.ds(start, n) if stride == 1 else pl.ds(start, n, stride=stride)


def _band_bias(nk, offset):
    qi = lax.broadcasted_iota(jnp.int32, (2 * QBLOCK, nk), 0) & (QBLOCK - 1)
    kj = lax.broadcasted_iota(jnp.int32, (2 * QBLOCK, nk), 1)
    dist = qi + offset - kj
    return jnp.where((dist >= 0) & (dist <= BAND), 0.0, NEG).astype(F32)


def _attn_prompt_kernel(q_ref, k_ref, v_ref, o_ref, o1_buf, l1_buf, o2_buf, l2_buf, s_buf, bias_buf, *, seq):
    step = pl.program_id(1)
    lane = lax.broadcasted_iota(jnp.int32, (QBLOCK, LANES), 1)
    first = lane < HEAD_DIM
    bias_buf[0] = _band_bias(2 * QBLOCK, 0)
    bias_buf[1] = _band_bias(2 * QBLOCK, QBLOCK)

    def scores(slot, hp, qstart, kstart, nk, dil, bias_sel):
        qrows, krows = _rows(qstart, QBLOCK, dil), _rows(kstart, nk, dil)
        q16 = q_ref[0, hp, qrows, :].astype(BF16)
        zero = jnp.zeros_like(q16)
        q2 = jnp.concatenate([jnp.where(first, q16, zero), jnp.where(first, zero, q16)], axis=0)
        s = _dot_nt(q2, k_ref[0, hp, krows, :].astype(BF16))
        s_buf[slot, hp, :, 0:nk] = s + bias_buf[bias_sel, :, 0:nk]

    def values(slot, hp, qstart, kstart, nk, dil, emit):
        qrows, krows = _rows(qstart, QBLOCK, dil), _rows(kstart, nk, dil)
        s = s_buf[slot, hp, :, 0:nk]
        m = jnp.max(s, axis=-1, keepdims=True)
        p = jnp.exp2(s - m).astype(BF16)
        v_ext = jnp.concatenate([v_ref[0, hp, krows, :].astype(BF16), jnp.ones((nk, LANES), BF16)], axis=1)
        o = _dot(p, v_ext)
        num = jnp.where(first, o[:QBLOCK, :LANES], o[QBLOCK:, :LANES])
        den = jnp.where(first, o[:QBLOCK, LANES:], o[QBLOCK:, LANES:])
        lse = jnp.where(first, m[:QBLOCK], m[QBLOCK:]) + jnp.log2(den)
        emit(qrows, hp, num / den, lse)

    def run(tasks, nk, dil, emit):
        assert len(tasks) % 4 == 0

        def stage(score_slot, i_score, value_slot, i_value):
            for d in range(2):
                for hp in range(2):
                    if i_score + d < len(tasks):
                        scores(score_slot + d, hp, *tasks[i_score + d][:2], nk, dil, tasks[i_score + d][2])
                    values(value_slot + d, hp, *tasks[i_value + d][:2], nk, dil, emit)

        for d in range(2):
            for hp in range(2):
                scores(d, hp, *tasks[d][:2], nk, dil, tasks[d][2])
        for i0 in range(0, len(tasks), 4):
            stage(2, i0 + 2, 0, i0)
            stage(0, i0 + 4, 2, i0 + 2)

    def store_to(o_buf, l_buf):
        def emit(qrows, hp, out, lse):
            o_buf[hp, qrows, :] = out
            l_buf[hp, qrows, :] = lse
        return emit

    def subsequence_tasks(dil):
        span = dil * QBLOCK
        return [(r + nb * span, r + max(nb - 1, 0) * span, min(nb, 1)) for r in range(dil) for nb in range(seq // span)]

    @pl.when(step == 0)
    def _():
        assert seq == 16 * QBLOCK
        run(subsequence_tasks(16), QBLOCK, 16, store_to(o2_buf, l2_buf))

    @pl.when(step == 1)
    def _():
        run(subsequence_tasks(4), 2 * QBLOCK, 4, store_to(o1_buf, l1_buf))

    @pl.when(step == 2)
    def _():
        def emit(qrows, hp, out0, lse0):
            lse1, lse2 = l1_buf[hp, qrows, :], l2_buf[hp, qrows, :]
            m = jnp.maximum(jnp.maximum(lse0, lse1), lse2)
            w0, w1, w2 = jnp.exp2(lse0 - m), jnp.exp2(lse1 - m), jnp.exp2(lse2 - m)
            num = w0 * out0 + w1 * o1_buf[hp, qrows, :] + w2 * o2_buf[hp, qrows, :]
            o_ref[0, qrows, hp * LANES:(hp + 1) * LANES] = (num / (w0 + w1 + w2)).astype(o_ref.dtype)

        run(subsequence_tasks(1), 2 * QBLOCK, 1, emit)


def _attention_prompt(q, k, v):
    b, _, s, _ = q.shape
    spec = pl.BlockSpec((1, 2, s, LANES), lambda i, j: (i, 2 - j, 0, 0))
    group_buf = pltpu.VMEM((2, s, LANES), F32)
    return pl.pallas_call(
        functools.partial(_attn_prompt_kernel, seq=s),
        grid=(b, 3),
        in_specs=[spec, spec, spec],
        out_specs=pl.BlockSpec((1, s, GROUP_WIDTH), lambda i, j: (i, 0, 0)),
        out_shape=jax.ShapeDtypeStruct((b, s, GROUP_WIDTH), BF16),
        scratch_shapes=[group_buf, group_buf, group_buf, group_buf,
                        pltpu.VMEM((4, 2, 2 * QBLOCK, 2 * QBLOCK), F32),
                        pltpu.VMEM((2, 2 * QBLOCK, 2 * QBLOCK), F32)],
        compiler_params=pltpu.CompilerParams(dimension_semantics=("arbitrary", "arbitrary"),
                                             vmem_limit_bytes=VMEM_LIMIT),
        name="attention_prompt",
    )(q, k, v)


def _attn_sample_parts(q_ref, k_ref, v_ref, c1_ref, c2_ref, c3_ref, o_ref, n1_ref, n2_ref, n3_ref, *, nt):
    nrow = HEADS_PER_GROUP * nt
    nt_shift = nt.bit_length() - 1
    assert nt == 1 << nt_shift
    row_head = lax.broadcasted_iota(jnp.int32, (nrow, GROUP_WIDTH), 0) >> nt_shift
    col_head = lax.broadcasted_iota(jnp.int32, (nrow, GROUP_WIDTH), 1) >> HEAD_SHIFT
    own = row_head == col_head
    col_head_t = lax.broadcasted_iota(jnp.int32, (nt, GROUP_WIDTH), 1) >> HEAD_SHIFT
    tail_lane = lax.broadcasted_iota(jnp.int32, (KV_ROWS, LANES), 1) >= LANES - nt

    def own_blocks(x):
        acc = jnp.zeros((nt, GROUP_WIDTH), F32)
        for h in range(HEADS_PER_GROUP):
            acc = acc + jnp.where(col_head_t == h, x[h * nt:(h + 1) * nt], 0.0)
        return acc

    def group(g):
        c_ref, n_ref = ((c1_ref, n1_ref), (c2_ref, n2_ref), (c3_ref, n3_ref))[g]
        win, dil = ATTN_PATTERNS[g]
        assert c_ref.shape == (1, KV_ROWS, win) and dil & (dil - 1) == 0
        cs = g * GROUP_WIDTH
        k_new = k_ref[:, cs:cs + GROUP_WIDTH]
        v_new = v_ref[:, cs:cs + GROUP_WIDTH]

        new_t = jnp.concatenate([jnp.zeros((LANES - nt, KV_ROWS), F32), jnp.concatenate([k_new, v_new], axis=1)],
                                axis=0).T
        ntiles = win // LANES
        rot = pltpu.roll(c_ref[0, :, 0:LANES], LANES - nt, axis=1)
        for j in range(ntiles):
            nxt = pltpu.roll(c_ref[0, :, (j + 1) * LANES:(j + 2) * LANES], LANES - nt, axis=1) if j + 1 < ntiles else new_t
            n_ref[0, :, j * LANES:(j + 1) * LANES] = jnp.where(tail_lane, nxt, rot)
            rot = nxt

        q = q_ref[:, cs:cs + GROUP_WIDTH]
        qbd = jnp.where(own, jnp.concatenate([q] * HEADS_PER_GROUP, axis=0), 0.0).astype(BF16)
        k_old = c_ref[0, 0:GROUP_WIDTH, :].astype(BF16)
        v_old = c_ref[0, GROUP_WIDTH:, :].astype(BF16)
        t_old = lax.broadcasted_iota(jnp.int32, (nrow, win), 0) & (nt - 1)
        i_old = lax.broadcasted_iota(jnp.int32, (nrow, win), 1)
        ok_old = (i_old >= t_old) & (((i_old - t_old) & (dil - 1)) == 0)
        t_new = lax.broadcasted_iota(jnp.int32, (nrow, nt), 0) & (nt - 1)
        i_new = lax.broadcasted_iota(jnp.int32, (nrow, nt), 1)
        ok_new = (i_new <= t_new) & (((t_new - i_new) & (dil - 1)) == 0)
        s_old = _dot(qbd, k_old) + jnp.where(ok_old, 0.0, NEG)
        s_new = _dot_nt(qbd, k_new.astype(BF16)) + jnp.where(ok_new, 0.0, NEG)
        m = jnp.maximum(jnp.max(s_old, axis=-1, keepdims=True), jnp.max(s_new, axis=-1, keepdims=True))
        p_old = jnp.exp2(s_old - m)
        p_new = jnp.exp2(s_new - m)
        den = jnp.sum(p_old, axis=-1, keepdims=True) + jnp.sum(p_new, axis=-1, keepdims=True)
        o = (_dot_nt(p_old.astype(BF16), v_old) + _dot(p_new.astype(BF16), v_new.astype(BF16))) / den
        return own_blocks(o), own_blocks(jnp.broadcast_to(m + jnp.log2(den), (nrow, GROUP_WIDTH)))

    def combine(parts):
        (o0, l0), (o1, l1), (o2, l2) = parts
        m = jnp.maximum(jnp.maximum(l0, l1), l2)
        w0, w1, w2 = jnp.exp2(l0 - m), jnp.exp2(l1 - m), jnp.exp2(l2 - m)
        o_ref[...] = (w0 * o0 + w1 * o1 + w2 * o2) / (w0 + w1 + w2)

    return group, combine


N_MERGE_IN = 9
N_SAMPLE_IN = 6


def _merge_mlp_kernel(*refs, ff_chunk, nt_sample):
    x_ref, ma_ref, sgb_ref, at_ref, wpb_ref, wo_ref, ln2_ref, wup_ref, wdn_ref = refs[:N_MERGE_IN]
    y_ref = refs[N_MERGE_IN + (N_SAMPLE_IN if nt_sample else 0)]
    pending = []
    if nt_sample:
        group, combine = _attn_sample_parts(*refs[N_MERGE_IN:N_MERGE_IN + N_SAMPLE_IN],
                                            *refs[N_MERGE_IN + N_SAMPLE_IN + 1:], nt=nt_sample)
        parts = [None] * len(ATTN_PATTERNS)

        def run_group(g):
            parts[g] = group(g)

        def run_combine():
            combine(parts)
        pending = [functools.partial(run_group, g) for g in (2, 1, 0)] + [run_combine]

    def issue_sample_piece():
        if pending:
            pending.pop(0)()

    issue_sample_piece()
    branch_b = _dot(at_ref[...].astype(BF16), wpb_ref[...])
    mixed = (ma_ref[...].astype(F32) + sgb_ref[...].astype(F32) * branch_b).astype(BF16)
    h = x_ref[...] + _dot(mixed, wo_ref[...])
    u = _rms_norm_bf16(h, ln2_ref[...])
    acc = h
    for c in range(0, D_FF, ff_chunk):
        z = jnp.maximum(_dot(u, wup_ref[:, c:c + ff_chunk]), 0.0)
        acc = acc + _dot((z * z).astype(BF16), wdn_ref[c:c + ff_chunk, :])
        issue_sample_piece()
    y_ref[...] = acc
    while pending:
        issue_sample_piece()


def _merge_mlp(x2d, ma, sgb, attn, w_pb, w_o, ln2, w_up, w_down, sample=None, *, tm=512, ff_chunk=1024):
    m = x2d.shape[0]
    tm = min(tm, m)
    assert m % tm == 0
    row = lambda width: pl.BlockSpec((tm, width), lambda i: (i, 0))
    args = [x2d, ma, sgb, attn, w_pb, w_o, ln2, w_up, w_down]
    in_specs = [row(D_MODEL), row(D_MODEL), row(D_MODEL), row(GROUP_WIDTH)] + [_const_spec(w.shape) for w in args[4:]]
    out_specs = [row(D_MODEL)]
    out_shape = [jax.ShapeDtypeStruct((m, D_MODEL), F32)]
    nt = 0
    if sample is not None:
        *sample_args, nt = sample
        q, c1, c2, c3 = sample_args[0], *sample_args[3:]
        nb = c1.shape[0]
        assert m // tm == nb and q.shape == (nb * nt, QKV_WIDTH), "one sample batch element per token tile"
        tok = pl.BlockSpec((nt, QKV_WIDTH), lambda i: (i, 0))
        cache = lambda c: pl.BlockSpec((1,) + c.shape[1:], lambda i: (i, 0, 0))
        args += sample_args
        in_specs += [tok, tok, tok, cache(c1), cache(c2), cache(c3)]
        out_specs += [pl.BlockSpec((nt, GROUP_WIDTH), lambda i: (i, 0)), cache(c1), cache(c2), cache(c3)]
        out_shape += [jax.ShapeDtypeStruct((nb * nt, GROUP_WIDTH), F32)] + [jax.ShapeDtypeStruct(c.shape, F32) for c in (c1, c2, c3)]
    out = pl.pallas_call(
        functools.partial(_merge_mlp_kernel, ff_chunk=ff_chunk, nt_sample=nt),
        grid=(m // tm,),
        in_specs=in_specs,
        out_specs=tuple(out_specs),
        out_shape=tuple(out_shape),
        compiler_params=pltpu.CompilerParams(dimension_semantics=("arbitrary",),
                                             vmem_limit_bytes=VMEM_LIMIT_FUSED if sample is not None else VMEM_LIMIT),
        name="merge_mlp",
    )(*args)
    return out if sample is not None else out[0]


def _position_minor(c):
    n, w = c.shape[1], c.shape[2]
    return jnp.transpose(c[0], (0, 2, 3, 4, 1)).reshape(n, KV_ROWS, w)


def _position_major(c):
    n, _, w = c.shape
    return jnp.transpose(c.reshape(n, 2, HEADS_PER_GROUP, HEAD_DIM, w), (0, 4, 1, 2, 3))[None]


def kernel(x_prompt, x_sample, state_pool, cache_kv1, cache_kv2, cache_kv3, ln1, w_in, q_norm, k_norm,
           pool_lin, pool_scale, w_pa, w_pb, w_o, ln2, w_up, w_down):
    depth = ln1.shape[0]
    assert depth == 1
    b, s, _ = x_prompt.shape
    nb, nt, _ = x_sample.shape

    w_in16, w_pa16, w_pb16 = w_in[0].astype(BF16), w_pa[0].astype(BF16), w_pb[0].astype(BF16)
    w_o16, w_up16, w_dn16 = w_o[0].astype(BF16), w_up[0].astype(BF16), w_down[0].astype(BF16)
    ln1v, ln2v = ln1[0].reshape(1, D_MODEL), ln2[0].reshape(1, D_MODEL)
    qn, kn = q_norm[0].reshape(1, QKV_WIDTH), k_norm[0].reshape(1, QKV_WIDTH)
    scale = pool_scale[0].reshape(1, POOL_WIDTH)
    lin = pool_lin[0].astype(BF16)
    zeros = jnp.zeros((POOL_GROUP_DIM, POOL_GROUP_DIM), BF16)
    lin2 = jnp.stack([jnp.block([[lin[0], zeros], [zeros, lin[1]]]), jnp.block([[lin[2], zeros], [zeros, lin[3]]])])

    q, k, v, ma, sgb, kv1, kv2, kv3, pool_p = _project_prompt(x_prompt, ln1v, w_in16, qn, kn, lin2, scale, w_pa16)
    xs = x_sample.reshape(nb * nt, D_MODEL)
    state_t = jnp.transpose(state_pool[0], (1, 0, 2))
    qs, ks, vs, mas, sgbs, pool_s = _project_sample(xs, state_t, ln1v, w_in16, qn, kn, lin2, scale, w_pa16, nb=nb, nt=nt)

    attn = _attention_prompt(q, k, v)
    caches = [_position_minor(c) for c in (cache_kv1, cache_kv2, cache_kv3)]
    y_prompt, attn_s, n1, n2, n3 = _merge_mlp(
        x_prompt.reshape(b * s, D_MODEL), ma.reshape(b * s, D_MODEL), sgb.reshape(b * s, D_MODEL),
        attn.reshape(b * s, GROUP_WIDTH), w_pb16, w_o16, ln2v, w_up16, w_dn16, sample=(qs, ks, vs, *caches, nt))
    y_prompt = y_prompt.reshape(b, s, D_MODEL)
    y_sample = _merge_mlp(xs, mas, sgbs, attn_s, w_pb16, w_o16, ln2v, w_up16, w_dn16).reshape(nb, nt, D_MODEL)

    return (y_prompt, y_sample,
            pool_p.reshape(1, b, POOL_STATE, POOL_WIDTH),
            _position_major(kv1), _position_major(kv2), _position_major(kv3),
            jnp.transpose(pool_s, (1, 0, 2))[None],
            _position_major(n1), _position_major(n2), _position_major(n3))
```

```python
import functools

import jax
import jax.numpy as jnp
from jax import lax
from jax.experimental import pallas as pl
from jax.experimental.pallas import tpu as pltpu

F32 = jnp.float32
BF16 = jnp.bfloat16

D_MODEL = 1024
POOL_WINDOWS = (2, 4, 8, 16)
POOL_WIDTH = 512
POOL_GROUP_DIM = 128
POOL_STATE = 15
POOL_HIST = 16
POOL_PAD = 8
ATTN_PATTERNS = ((128, 1), (512, 4), (2048, 16))
BAND = 128
HEAD_DIM = 64
HEAD_SHIFT = 6
HEADS_PER_GROUP = 4
GROUP_WIDTH = HEADS_PER_GROUP * HEAD_DIM
KV_ROWS = 2 * GROUP_WIDTH
QKV_WIDTH = 3 * GROUP_WIDTH
N_PAIRS = QKV_WIDTH // 128
D_FF = 4096
PAST_LEN = 8192
EPS = 1e-6
COL_A, COL_Q, COL_K, COL_V, COL_GA, COL_GB = 0, 512, 1280, 2048, 2816, 3840
Q_SCALE = HEAD_DIM ** -0.5 * 1.4426950408889634
QBLOCK = 128
NEG = -1e30
LANES = 128
VMEM_LIMIT = 56 * 1024 * 1024
VMEM_LIMIT_FUSED = 62 * 1024 * 1024


def _dot(a, b):
    return jnp.dot(a, b, preferred_element_type=F32)


def _dot_nt(a, b):
    return lax.dot_general(a, b, (((1,), (1,)), ((), ())), preferred_element_type=F32)


def _rms_norm_bf16(x, w):
    ms = jnp.mean(x * x, axis=-1, keepdims=True)
    return ((x * lax.rsqrt(ms + EPS)) * w).astype(BF16)


def _sigmoid(x):
    return 1.0 / (1.0 + jnp.exp(-x))


def _head_norm(y, w):
    first = lax.broadcasted_iota(jnp.int32, (1, LANES), 1) < HEAD_DIM
    outs = []
    for g in range(3):
        halves = []
        for hp in range(2):
            cols = slice(g * GROUP_WIDTH + hp * LANES, g * GROUP_WIDTH + (hp + 1) * LANES)
            yg = y[:, cols]
            sq = yg * yg
            s0 = jnp.sum(jnp.where(first, sq, 0.0), axis=-1, keepdims=True)
            s1 = jnp.sum(jnp.where(first, 0.0, sq), axis=-1, keepdims=True)
            ms = jnp.where(first, s0, s1) * (1.0 / HEAD_DIM)
            halves.append((yg * lax.rsqrt(ms + EPS)) * w[:, cols])
        outs.append(jnp.concatenate(halves, axis=1))
    return outs


def _project_qkv(u, w_ref, qn_ref, kn_ref, store):
    q = _head_norm(_dot(u, w_ref[:, COL_Q:COL_Q + QKV_WIDTH]), qn_ref[...] * Q_SCALE)
    k = _head_norm(_dot(u, w_ref[:, COL_K:COL_K + QKV_WIDTH]), kn_ref[...])
    v = _dot(u, w_ref[:, COL_V:COL_V + QKV_WIDTH])
    for g in range(3):
        store(g, q[g], k[g], v[:, g * GROUP_WIDTH:(g + 1) * GROUP_WIDTH])


def _pool_branch_and_gates(u, diff, w_ref, lin_ref, scale_ref, wpa_ref, ma_out, sgb_out):
    d16 = diff.astype(BF16)
    z = jnp.concatenate([_dot(d16[:, 0:256], lin_ref[0]), _dot(d16[:, 256:512], lin_ref[1])], axis=1)
    a_mix = (z * scale_ref[...]).astype(BF16)
    branch_a = _dot(a_mix, wpa_ref[...])
    g_a = _dot(u, w_ref[:, COL_GA:COL_GA + D_MODEL])
    ma_out[...] = (_sigmoid(g_a) * branch_a).astype(BF16)
    g_b = _dot(u, w_ref[:, COL_GB:COL_GB + D_MODEL])
    sgb_out[...] = _sigmoid(g_b).astype(BF16)


def _proj_prompt_kernel(x_ref, ln1_ref, w_ref, qn_ref, kn_ref, lin_ref, scale_ref, wpa_ref,
                        q_out, k_out, v_out, ma_out, sgb_out, kv1_out, kv2_out, kv3_out, pool_out,
                        lvl, *, tm):
    t = pl.program_id(1)
    last = pl.num_programs(1) - 1
    nlev = len(POOL_WINDOWS)
    top = POOL_PAD + POOL_HIST

    @pl.when(t == 0)
    def _():
        lvl[:, 0:top, :] = jnp.zeros((nlev, top, POOL_WIDTH), F32)

    u = _rms_norm_bf16(x_ref[0], ln1_ref[...])

    def store(g, q, k, v):
        for hp in range(2):
            cols = slice(hp * LANES, (hp + 1) * LANES)
            q_out[0, 2 * g + hp] = q[:, cols]
            k_out[0, 2 * g + hp] = k[:, cols]
            v_out[0, 2 * g + hp] = v[:, cols]
        if g == 2:
            kv3_out[0, 0:GROUP_WIDTH, :] = k.T
            kv3_out[0, GROUP_WIDTH:, :] = v.T

    _project_qkv(u, w_ref, qn_ref, kn_ref, store)

    lvl[0, top:top + tm, :] = _dot(u, w_ref[:, COL_A:COL_A + POOL_WIDTH])
    pos = t * tm + lax.broadcasted_iota(jnp.int32, (tm, 1), 0)
    diffs = []
    for j, w in enumerate(POOL_WINDOWS):
        assert w == 2 << j and w // 2 <= POOL_PAD
        c0 = j * POOL_GROUP_DIM
        rows = slice(POOL_PAD, top + tm)
        prev = slice(POOL_PAD - w // 2, top + tm - w // 2)
        acc = lvl[j, rows, c0:] + lvl[j, prev, c0:]
        if j + 1 < nlev:
            lvl[j + 1, rows, c0 + POOL_GROUP_DIM:] = acc[:, POOL_GROUP_DIM:]
        cnt = jnp.minimum(pos + 1, w).astype(F32)
        diffs.append(acc[POOL_HIST:, 0:POOL_GROUP_DIM] / cnt - lvl[0, top:top + tm, c0:c0 + POOL_GROUP_DIM])
    diff = jnp.concatenate(diffs, axis=1)

    _pool_branch_and_gates(u, diff, w_ref, lin_ref, scale_ref, wpa_ref, ma_out.at[0], sgb_out.at[0])

    @pl.when(t == last)
    def _():
        def pair_rows(ref, g, r0):
            return jnp.concatenate([ref[0, 2 * g, r0:tm, :], ref[0, 2 * g + 1, r0:tm, :]], axis=1)
        kv2_out[0, 0:GROUP_WIDTH, :] = pair_rows(k_out, 1, 0).T
        kv2_out[0, GROUP_WIDTH:, :] = pair_rows(v_out, 1, 0).T
        kv1_out[0, 0:GROUP_WIDTH, :] = pair_rows(k_out, 0, tm - 128).T
        kv1_out[0, GROUP_WIDTH:, :] = pair_rows(v_out, 0, tm - 128).T
        pool_out[0] = lvl[0, top + tm - POOL_STATE:top + tm, :]

    lvl[0, POOL_PAD:top, :] = lvl[0, tm + POOL_PAD:tm + top, :]


def _const_spec(shape):
    nd = len(shape)
    return pl.BlockSpec(shape, lambda *_: (0,) * nd, pipeline_mode=pl.Buffered(1))


def _project_prompt(x, ln1, w_in, qn, kn, lin2, scale, w_pa, *, tm=512):
    b, s, _ = x.shape
    nt = s // tm
    assert s % tm == 0 and tm == ATTN_PATTERNS[1][0] and s == ATTN_PATTERNS[2][0]
    row = lambda width: pl.BlockSpec((1, tm, width), lambda i, j: (i, j, 0))
    tail = lambda rows, width: pl.BlockSpec((1, rows, width), lambda i, j: (i, 0, 0))
    pairs = pl.BlockSpec((1, N_PAIRS, tm, LANES), lambda i, j: (i, 0, j, 0))
    out_shape = (
        jax.ShapeDtypeStruct((b, N_PAIRS, s, LANES), F32),
        jax.ShapeDtypeStruct((b, N_PAIRS, s, LANES), F32),
        jax.ShapeDtypeStruct((b, N_PAIRS, s, LANES), F32),
        jax.ShapeDtypeStruct((b, s, D_MODEL), BF16),
        jax.ShapeDtypeStruct((b, s, D_MODEL), BF16),
        jax.ShapeDtypeStruct((b, KV_ROWS, 128), F32),
        jax.ShapeDtypeStruct((b, KV_ROWS, 512), F32),
        jax.ShapeDtypeStruct((b, KV_ROWS, s), F32),
        jax.ShapeDtypeStruct((b, POOL_STATE, POOL_WIDTH), F32),
    )
    return pl.pallas_call(
        functools.partial(_proj_prompt_kernel, tm=tm),
        grid=(b, nt),
        in_specs=[row(D_MODEL), _const_spec(ln1.shape), _const_spec(w_in.shape), _const_spec(qn.shape),
                  _const_spec(kn.shape), _const_spec(lin2.shape), _const_spec(scale.shape),
                  _const_spec(w_pa.shape)],
        out_specs=(pairs, pairs, pairs, row(D_MODEL), row(D_MODEL),
                   tail(KV_ROWS, 128), tail(KV_ROWS, 512),
                   pl.BlockSpec((1, KV_ROWS, tm), lambda i, j: (i, 0, j)),
                   tail(POOL_STATE, POOL_WIDTH)),
        out_shape=out_shape,
        scratch_shapes=[pltpu.VMEM((len(POOL_WINDOWS), POOL_PAD + POOL_HIST + tm, POOL_WIDTH), F32)],
        compiler_params=pltpu.CompilerParams(dimension_semantics=("arbitrary", "arbitrary"),
                                             vmem_limit_bytes=VMEM_LIMIT),
        name="project_prompt",
    )(x, ln1, w_in, qn, kn, lin2, scale, w_pa)


def _proj_sample_kernel(x_ref, st_ref, ln1_ref, w_ref, qn_ref, kn_ref, lin_ref, scale_ref, wpa_ref,
                        q_out, k_out, v_out, ma_out, sgb_out, pool_out, abuf, dbuf, *, nb, nt):
    u = _rms_norm_bf16(x_ref[...], ln1_ref[...])

    def store(g, q, k, v):
        cols = slice(g * GROUP_WIDTH, (g + 1) * GROUP_WIDTH)
        q_out[:, cols] = q
        k_out[:, cols] = k
        v_out[:, cols] = v

    _project_qkv(u, w_ref, qn_ref, kn_ref, store)

    a = _dot(u, w_ref[:, COL_A:COL_A + POOL_WIDTH])
    for g in range(len(POOL_WINDOWS)):
        abuf[g] = a[:, g * POOL_GROUP_DIM:(g + 1) * POOL_GROUP_DIM]

    def slab(j, g):
        if j < POOL_STATE:
            return st_ref[j, :, g * POOL_GROUP_DIM:(g + 1) * POOL_GROUP_DIM]
        return abuf[g, pl.ds(j - POOL_STATE, nb, stride=nt), :]

    for t in range(nt):
        for g, w in enumerate(POOL_WINDOWS):
            a_t = slab(POOL_STATE + t, g)
            acc = a_t
            for s in range(1, w):
                acc = acc + slab(POOL_STATE + t - s, g)
            cnt = float(min(PAST_LEN + t + 1, w))
            dbuf[g, pl.ds(t, nb, stride=nt), :] = acc / cnt - a_t

    for j in range(POOL_STATE):
        pool_out[j] = jnp.concatenate([slab(j + nt, g) for g in range(len(POOL_WINDOWS))], axis=1)

    diff = jnp.concatenate([dbuf[g] for g in range(len(POOL_WINDOWS))], axis=1)
    _pool_branch_and_gates(u, diff, w_ref, lin_ref, scale_ref, wpa_ref, ma_out, sgb_out)


def _project_sample(x2d, state_t, ln1, w_in, qn, kn, lin2, scale, w_pa, *, nb, nt):
    m = nb * nt
    assert nt <= POOL_STATE and x2d.shape == (m, D_MODEL) and state_t.shape == (POOL_STATE, nb, POOL_WIDTH)
    args = (x2d, state_t, ln1, w_in, qn, kn, lin2, scale, w_pa)
    out_shape = (
        jax.ShapeDtypeStruct((m, QKV_WIDTH), F32),
        jax.ShapeDtypeStruct((m, QKV_WIDTH), F32),
        jax.ShapeDtypeStruct((m, QKV_WIDTH), F32),
        jax.ShapeDtypeStruct((m, D_MODEL), BF16),
        jax.ShapeDtypeStruct((m, D_MODEL), BF16),
        jax.ShapeDtypeStruct((POOL_STATE, nb, POOL_WIDTH), F32),
    )
    group_buf = pltpu.VMEM((len(POOL_WINDOWS), m, POOL_GROUP_DIM), F32)
    return pl.pallas_call(
        functools.partial(_proj_sample_kernel, nb=nb, nt=nt),
        grid=(1,),
        in_specs=[_const_spec(a.shape) for a in args],
        out_specs=tuple(pl.BlockSpec(o.shape, lambda i, nd=len(o.shape): (0,) * nd) for o in out_shape),
        out_shape=out_shape,
        scratch_shapes=[group_buf, group_buf],
        compiler_params=pltpu.CompilerParams(dimension_semantics=("arbitrary",), vmem_limit_bytes=VMEM_LIMIT),
        name="project_sample",
    )(*args)


def _rows(start, n, stride):
    return pl.ds(start, n) if stride == 1 else pl.ds(start, n, stride=stride)


def _band_bias(nk, offset):
    qi = lax.broadcasted_iota(jnp.int32, (2 * QBLOCK, nk), 0) & (QBLOCK - 1)
    kj = lax.broadcasted_iota(jnp.int32, (2 * QBLOCK, nk), 1)
    dist = qi + offset - kj
    return jnp.where((dist >= 0) & (dist <= BAND), 0.0, NEG).astype(F32)


def _attn_prompt_kernel(q_ref, k_ref, v_ref, o_ref, o1_buf, l1_buf, o2_buf, l2_buf, s_buf, bias_buf, *, seq):
    step = pl.program_id(1)
    lane = lax.broadcasted_iota(jnp.int32, (QBLOCK, LANES), 1)
    first = lane < HEAD_DIM
    bias_buf[0] = _band_bias(2 * QBLOCK, 0)
    bias_buf[1] = _band_bias(2 * QBLOCK, QBLOCK)

    def scores(slot, hp, qstart, kstart, nk, dil, bias_sel):
        qrows, krows = _rows(qstart, QBLOCK, dil), _rows(kstart, nk, dil)
        q16 = q_ref[0, hp, qrows, :].astype(BF16)
        zero = jnp.zeros_like(q16)
        q2 = jnp.concatenate([jnp.where(first, q16, zero), jnp.where(first, zero, q16)], axis=0)
        s = _dot_nt(q2, k_ref[0, hp, krows, :].astype(BF16))
        s_buf[slot, hp, :, 0:nk] = s + bias_buf[bias_sel, :, 0:nk]

    def values(slot, hp, qstart, kstart, nk, dil, emit):
        qrows, krows = _rows(qstart, QBLOCK, dil), _rows(kstart, nk, dil)
        s = s_buf[slot, hp, :, 0:nk]
        m = jnp.max(s, axis=-1, keepdims=True)
        p = jnp.exp2(s - m).astype(BF16)
        v_ext = jnp.concatenate([v_ref[0, hp, krows, :].astype(BF16), jnp.ones((nk, LANES), BF16)], axis=1)
        o = _dot(p, v_ext)
        num = jnp.where(first, o[:QBLOCK, :LANES], o[QBLOCK:, :LANES])
        den = jnp.where(first, o[:QBLOCK, LANES:], o[QBLOCK:, LANES:])
        lse = jnp.where(first, m[:QBLOCK], m[QBLOCK:]) + jnp.log2(den)
        emit(qrows, hp, num / den, lse)

    def run(tasks, nk, dil, emit):
        assert len(tasks) % 4 == 0

        def stage(score_slot, i_score, value_slot, i_value):
            for d in range(2):
                for hp in range(2):
                    if i_score + d < len(tasks):
                        scores(score_slot + d, hp, *tasks[i_score + d][:2], nk, dil, tasks[i_score + d][2])
                    values(value_slot + d, hp, *tasks[i_value + d][:2], nk, dil, emit)

        for d in range(2):
            for hp in range(2):
                scores(d, hp, *tasks[d][:2], nk, dil, tasks[d][2])
        for i0 in range(0, len(tasks), 4):
            stage(2, i0 + 2, 0, i0)
            stage(0, i0 + 4, 2, i0 + 2)

    def store_to(o_buf, l_buf):
        def emit(qrows, hp, out, lse):
            o_buf[hp, qrows, :] = out
            l_buf[hp, qrows, :] = lse
        return emit

    def subsequence_tasks(dil):
        span = dil * QBLOCK
        return [(r + nb * span, r + max(nb - 1, 0) * span, min(nb, 1)) for r in range(dil) for nb in range(seq // span)]

    @pl.when(step == 0)
    def _():
        assert seq == 16 * QBLOCK
        run(subsequence_tasks(16), QBLOCK, 16, store_to(o2_buf, l2_buf))

    @pl.when(step == 1)
    def _():
        run(subsequence_tasks(4), 2 * QBLOCK, 4, store_to(o1_buf, l1_buf))

    @pl.when(step == 2)
    def _():
        def emit(qrows, hp, out0, lse0):
            lse1, lse2 = l1_buf[hp, qrows, :], l2_buf[hp, qrows, :]
            m = jnp.maximum(jnp.maximum(lse0, lse1), lse2)
            w0, w1, w2 = jnp.exp2(lse0 - m), jnp.exp2(lse1 - m), jnp.exp2(lse2 - m)
            num = w0 * out0 + w1 * o1_buf[hp, qrows, :] + w2 * o2_buf[hp, qrows, :]
            o_ref[0, qrows, hp * LANES:(hp + 1) * LANES] = (num / (w0 + w1 + w2)).astype(o_ref.dtype)

        run(subsequence_tasks(1), 2 * QBLOCK, 1, emit)


def _attention_prompt(q, k, v):
    b, _, s, _ = q.shape
    spec = pl.BlockSpec((1, 2, s, LANES), lambda i, j: (i, 2 - j, 0, 0))
    group_buf = pltpu.VMEM((2, s, LANES), F32)
    return pl.pallas_call(
        functools.partial(_attn_prompt_kernel, seq=s),
        grid=(b, 3),
        in_specs=[spec, spec, spec],
        out_specs=pl.BlockSpec((1, s, GROUP_WIDTH), lambda i, j: (i, 0, 0)),
        out_shape=jax.ShapeDtypeStruct((b, s, GROUP_WIDTH), BF16),
        scratch_shapes=[group_buf, group_buf, group_buf, group_buf,
                        pltpu.VMEM((4, 2, 2 * QBLOCK, 2 * QBLOCK), F32),
                        pltpu.VMEM((2, 2 * QBLOCK, 2 * QBLOCK), F32)],
        compiler_params=pltpu.CompilerParams(dimension_semantics=("arbitrary", "arbitrary"),
                                             vmem_limit_bytes=VMEM_LIMIT),
        name="attention_prompt",
    )(q, k, v)


def _attn_sample_parts(q_ref, k_ref, v_ref, c1_ref, c2_ref, c3_ref, o_ref, n1_ref, n2_ref, n3_ref, *, nt):
    nrow = HEADS_PER_GROUP * nt
    nt_shift = nt.bit_length() - 1
    assert nt == 1 << nt_shift
    row_head = lax.broadcasted_iota(jnp.int32, (nrow, GROUP_WIDTH), 0) >> nt_shift
    col_head = lax.broadcasted_iota(jnp.int32, (nrow, GROUP_WIDTH), 1) >> HEAD_SHIFT
    own = row_head == col_head
    col_head_t = lax.broadcasted_iota(jnp.int32, (nt, GROUP_WIDTH), 1) >> HEAD_SHIFT
    tail_lane = lax.broadcasted_iota(jnp.int32, (KV_ROWS, LANES), 1) >= LANES - nt

    def own_blocks(x):
        acc = jnp.zeros((nt, GROUP_WIDTH), F32)
        for h in range(HEADS_PER_GROUP):
            acc = acc + jnp.where(col_head_t == h, x[h * nt:(h + 1) * nt], 0.0)
        return acc

    def group(g):
        c_ref, n_ref = ((c1_ref, n1_ref), (c2_ref, n2_ref), (c3_ref, n3_ref))[g]
        win, dil = ATTN_PATTERNS[g]
        assert c_ref.shape == (1, KV_ROWS, win) and dil & (dil - 1) == 0
        cs = g * GROUP_WIDTH
        k_new = k_ref[:, cs:cs + GROUP_WIDTH]
        v_new = v_ref[:, cs:cs + GROUP_WIDTH]

        new_t = jnp.concatenate([jnp.zeros((LANES - nt, KV_ROWS), F32), jnp.concatenate([k_new, v_new], axis=1)],
                                axis=0).T
        ntiles = win // LANES
        rot = pltpu.roll(c_ref[0, :, 0:LANES], LANES - nt, axis=1)
        for j in range(ntiles):
            nxt = pltpu.roll(c_ref[0, :, (j + 1) * LANES:(j + 2) * LANES], LANES - nt, axis=1) if j + 1 < ntiles else new_t
            n_ref[0, :, j * LANES:(j + 1) * LANES] = jnp.where(tail_lane, nxt, rot)
            rot = nxt

        q = q_ref[:, cs:cs + GROUP_WIDTH]
        qbd = jnp.where(own, jnp.concatenate([q] * HEADS_PER_GROUP, axis=0), 0.0).astype(BF16)
        k_old = c_ref[0, 0:GROUP_WIDTH, :].astype(BF16)
        v_old = c_ref[0, GROUP_WIDTH:, :].astype(BF16)
        t_old = lax.broadcasted_iota(jnp.int32, (nrow, win), 0) & (nt - 1)
        i_old = lax.broadcasted_iota(jnp.int32, (nrow, win), 1)
        ok_old = (i_old >= t_old) & (((i_old - t_old) & (dil - 1)) == 0)
        t_new = lax.broadcasted_iota(jnp.int32, (nrow, nt), 0) & (nt - 1)
        i_new = lax.broadcasted_iota(jnp.int32, (nrow, nt), 1)
        ok_new = (i_new <= t_new) & (((t_new - i_new) & (dil - 1)) == 0)
        s_old = _dot(qbd, k_old) + jnp.where(ok_old, 0.0, NEG)
        s_new = _dot_nt(qbd, k_new.astype(BF16)) + jnp.where(ok_new, 0.0, NEG)
        m = jnp.maximum(jnp.max(s_old, axis=-1, keepdims=True), jnp.max(s_new, axis=-1, keepdims=True))
        p_old = jnp.exp2(s_old - m)
        p_new = jnp.exp2(s_new - m)
        den = jnp.sum(p_old, axis=-1, keepdims=True) + jnp.sum(p_new, axis=-1, keepdims=True)
        o = (_dot_nt(p_old.astype(BF16), v_old) + _dot(p_new.astype(BF16), v_new.astype(BF16))) / den
        return own_blocks(o), own_blocks(jnp.broadcast_to(m + jnp.log2(den), (nrow, GROUP_WIDTH)))

    def combine(parts):
        (o0, l0), (o1, l1), (o2, l2) = parts
        m = jnp.maximum(jnp.maximum(l0, l1), l2)
        w0, w1, w2 = jnp.exp2(l0 - m), jnp.exp2(l1 - m), jnp.exp2(l2 - m)
        o_ref[...] = (w0 * o0 + w1 * o1 + w2 * o2) / (w0 + w1 + w2)

    return group, combine


N_MERGE_IN = 9
N_SAMPLE_IN = 6


def _merge_mlp_kernel(*refs, ff_chunk, nt_sample):
    x_ref, ma_ref, sgb_ref, at_ref, wpb_ref, wo_ref, ln2_ref, wup_ref, wdn_ref = refs[:N_MERGE_IN]
    y_ref = refs[N_MERGE_IN + (N_SAMPLE_IN if nt_sample else 0)]
    pending = []
    if nt_sample:
        group, combine = _attn_sample_parts(*refs[N_MERGE_IN:N_MERGE_IN + N_SAMPLE_IN],
                                            *refs[N_MERGE_IN + N_SAMPLE_IN + 1:], nt=nt_sample)
        parts = [None] * len(ATTN_PATTERNS)

        def run_group(g):
            parts[g] = group(g)

        def run_combine():
            combine(parts)
        pending = [functools.partial(run_group, g) for g in (2, 1, 0)] + [run_combine]

    def issue_sample_piece():
        if pending:
            pending.pop(0)()

    issue_sample_piece()
    branch_b = _dot(at_ref[...].astype(BF16), wpb_ref[...])
    mixed = (ma_ref[...].astype(F32) + sgb_ref[...].astype(F32) * branch_b).astype(BF16)
    h = x_ref[...] + _dot(mixed, wo_ref[...])
    u = _rms_norm_bf16(h, ln2_ref[...])
    acc = h
    for c in range(0, D_FF, ff_chunk):
        z = jnp.maximum(_dot(u, wup_ref[:, c:c + ff_chunk]), 0.0)
        acc = acc + _dot((z * z).astype(BF16), wdn_ref[c:c + ff_chunk, :])
        issue_sample_piece()
    y_ref[...] = acc
    while pending:
        issue_sample_piece()


def _merge_mlp(x2d, ma, sgb, attn, w_pb, w_o, ln2, w_up, w_down, sample=None, *, tm=512, ff_chunk=1024):
    m = x2d.shape[0]
    tm = min(tm, m)
    assert m % tm == 0
    row = lambda width: pl.BlockSpec((tm, width), lambda i: (i, 0))
    args = [x2d, ma, sgb, attn, w_pb, w_o, ln2, w_up, w_down]
    in_specs = [row(D_MODEL), row(D_MODEL), row(D_MODEL), row(GROUP_WIDTH)] + [_const_spec(w.shape) for w in args[4:]]
    out_specs = [row(D_MODEL)]
    out_shape = [jax.ShapeDtypeStruct((m, D_MODEL), F32)]
    nt = 0
    if sample is not None:
        *sample_args, nt = sample
        q, c1, c2, c3 = sample_args[0], *sample_args[3:]
        nb = c1.shape[0]
        assert m // tm == nb and q.shape == (nb * nt, QKV_WIDTH), "one sample batch element per token tile"
        tok = pl.BlockSpec((nt, QKV_WIDTH), lambda i: (i, 0))
        cache = lambda c: pl.BlockSpec((1,) + c.shape[1:], lambda i: (i, 0, 0))
        args += sample_args
        in_specs += [tok, tok, tok, cache(c1), cache(c2), cache(c3)]
        out_specs += [pl.BlockSpec((nt, GROUP_WIDTH), lambda i: (i, 0)), cache(c1), cache(c2), cache(c3)]
        out_shape += [jax.ShapeDtypeStruct((nb * nt, GROUP_WIDTH), F32)] + [jax.ShapeDtypeStruct(c.shape, F32) for c in (c1, c2, c3)]
    out = pl.pallas_call(
        functools.partial(_merge_mlp_kernel, ff_chunk=ff_chunk, nt_sample=nt),
        grid=(m // tm,),
        in_specs=in_specs,
        out_specs=tuple(out_specs),
        out_shape=tuple(out_shape),
        compiler_params=pltpu.CompilerParams(dimension_semantics=("arbitrary",),
                                             vmem_limit_bytes=VMEM_LIMIT_FUSED if sample is not None else VMEM_LIMIT),
        name="merge_mlp",
    )(*args)
    return out if sample is not None else out[0]


def _position_minor(c):
    n, w = c.shape[1], c.shape[2]
    return jnp.transpose(c[0], (0, 2, 3, 4, 1)).reshape(n, KV_ROWS, w)


def _position_major(c):
    n, _, w = c.shape
    return jnp.transpose(c.reshape(n, 2, HEADS_PER_GROUP, HEAD_DIM, w), (0, 4, 1, 2, 3))[None]


def kernel(x_prompt, x_sample, state_pool, cache_kv1, cache_kv2, cache_kv3, ln1, w_in, q_norm, k_norm,
           pool_lin, pool_scale, w_pa, w_pb, w_o, ln2, w_up, w_down):
    depth = ln1.shape[0]
    assert depth == 1
    b, s, _ = x_prompt.shape
    nb, nt, _ = x_sample.shape

    w_in16, w_pa16, w_pb16 = w_in[0].astype(BF16), w_pa[0].astype(BF16), w_pb[0].astype(BF16)
    w_o16, w_up16, w_dn16 = w_o[0].astype(BF16), w_up[0].astype(BF16), w_down[0].astype(BF16)
    ln1v, ln2v = ln1[0].reshape(1, D_MODEL), ln2[0].reshape(1, D_MODEL)
    qn, kn = q_norm[0].reshape(1, QKV_WIDTH), k_norm[0].reshape(1, QKV_WIDTH)
    scale = pool_scale[0].reshape(1, POOL_WIDTH)
    lin = pool_lin[0].astype(BF16)
    zeros = jnp.zeros((POOL_GROUP_DIM, POOL_GROUP_DIM), BF16)
    lin2 = jnp.stack([jnp.block([[lin[0], zeros], [zeros, lin[1]]]), jnp.block([[lin[2], zeros], [zeros, lin[3]]])])

    q, k, v, ma, sgb, kv1, kv2, kv3, pool_p = _project_prompt(x_prompt, ln1v, w_in16, qn, kn, lin2, scale, w_pa16)
    xs = x_sample.reshape(nb * nt, D_MODEL)
    state_t = jnp.transpose(state_pool[0], (1, 0, 2))
    qs, ks, vs, mas, sgbs, pool_s = _project_sample(xs, state_t, ln1v, w_in16, qn, kn, lin2, scale, w_pa16, nb=nb, nt=nt)

    attn = _attention_prompt(q, k, v)
    caches = [_position_minor(c) for c in (cache_kv1, cache_kv2, cache_kv3)]
    y_prompt, attn_s, n1, n2, n3 = _merge_mlp(
        x_prompt.reshape(b * s, D_MODEL), ma.reshape(b * s, D_MODEL), sgb.reshape(b * s, D_MODEL),
        attn.reshape(b * s, GROUP_WIDTH), w_pb16, w_o16, ln2v, w_up16, w_dn16, sample=(qs, ks, vs, *caches, nt))
    y_prompt = y_prompt.reshape(b, s, D_MODEL)
    y_sample = _merge_mlp(xs, mas, sgbs, attn_s, w_pb16, w_o16, ln2v, w_up16, w_dn16).reshape(nb, nt, D_MODEL)

    return (y_prompt, y_sample,
            pool_p.reshape(1, b, POOL_STATE, POOL_WIDTH),
            _position_major(kv1), _position_major(kv2), _position_major(kv3),
            jnp.transpose(pool_s, (1, 0, 2))[None],
            _position_major(n1), _position_major(n2), _position_major(n3))
```

```python
import functools

import jax
import jax.numpy as jnp
from jax import lax
from jax.experimental import pallas as pl
from jax.experimental.pallas import tpu as pltpu

F32 = jnp.float32
BF16 = jnp.bfloat16

D_MODEL = 1024
POOL_WINDOWS = (2, 4, 8, 16)
POOL_WIDTH = 512
POOL_GROUP_DIM = 128
POOL_STATE = 15
POOL_HIST = 16
POOL_PAD = 8
ATTN_PATTERNS = ((128, 1), (512, 4), (2048, 16))
BAND = 128
HEAD_DIM = 64
HEAD_SHIFT = 6
HEADS_PER_GROUP = 4
GROUP_WIDTH = HEADS_PER_GROUP * HEAD_DIM
KV_ROWS = 2 * GROUP_WIDTH
QKV_WIDTH = 3 * GROUP_WIDTH
N_PAIRS = QKV_WIDTH // 128
Q_SLAB, K_SLAB, V_SLAB = 0, N_PAIRS, 2 * N_PAIRS
D_FF = 4096
PAST_LEN = 8192
EPS = 1e-6
COL_A, COL_Q, COL_K, COL_V, COL_GA, COL_GB = 0, 512, 1280, 2048, 2816, 3840
Q_SCALE = HEAD_DIM ** -0.5 * 1.4426950408889634
QBLOCK = 128
NEG = -1e30
LANES = 128
VMEM_LIMIT = 56 * 1024 * 1024
VMEM_LIMIT_FUSED = 62 * 1024 * 1024


def _dot(a, b):
    return jnp.dot(a, b, preferred_element_type=F32)


def _dot_nt(a, b):
    return lax.dot_general(a, b, (((1,), (1,)), ((), ())), preferred_element_type=F32)


def _rms_norm_bf16(x, w):
    ms = jnp.mean(x * x, axis=-1, keepdims=True)
    return ((x * lax.rsqrt(ms + EPS)) * w).astype(BF16)


def _sigmoid(x):
    return 1.0 / (1.0 + jnp.exp(-x))


def _head_norm(y, w):
    first = lax.broadcasted_iota(jnp.int32, (1, LANES), 1) < HEAD_DIM
    outs = []
    for g in range(3):
        halves = []
        for hp in range(2):
            cols = slice(g * GROUP_WIDTH + hp * LANES, g * GROUP_WIDTH + (hp + 1) * LANES)
            yg = y[:, cols]
            sq = yg * yg
            s0 = jnp.sum(jnp.where(first, sq, 0.0), axis=-1, keepdims=True)
            s1 = jnp.sum(jnp.where(first, 0.0, sq), axis=-1, keepdims=True)
            ms = jnp.where(first, s0, s1) * (1.0 / HEAD_DIM)
            halves.append((yg * lax.rsqrt(ms + EPS)) * w[:, cols])
        outs.append(jnp.concatenate(halves, axis=1))
    return outs


def _project_qkv(u, w_ref, qn_ref, kn_ref, store):
    q = _head_norm(_dot(u, w_ref[:, COL_Q:COL_Q + QKV_WIDTH]), qn_ref[...] * Q_SCALE)
    k = _head_norm(_dot(u, w_ref[:, COL_K:COL_K + QKV_WIDTH]), kn_ref[...])
    v = _dot(u, w_ref[:, COL_V:COL_V + QKV_WIDTH])
    for g in range(3):
        store(g, q[g], k[g], v[:, g * GROUP_WIDTH:(g + 1) * GROUP_WIDTH])


def _pool_branch_and_gates(u, diff, w_ref, lin_ref, scale_ref, wpa_ref, gates_out):
    d16 = diff.astype(BF16)
    half = POOL_WIDTH // 2
    z = jnp.concatenate([_dot(d16[:, 0:half], lin_ref[0]), _dot(d16[:, half:], lin_ref[1])], axis=1)
    a_mix = (z * scale_ref[...]).astype(BF16)
    branch_a = _dot(a_mix, wpa_ref[...])
    g_a = _dot(u, w_ref[:, COL_GA:COL_GA + D_MODEL])
    gates_out[:, 0:D_MODEL] = (_sigmoid(g_a) * branch_a).astype(BF16)
    g_b = _dot(u, w_ref[:, COL_GB:COL_GB + D_MODEL])
    gates_out[:, D_MODEL:] = _sigmoid(g_b).astype(BF16)


def _proj_prompt_kernel(x_ref, ln1_ref, w_ref, qn_ref, kn_ref, lin_ref, scale_ref, wpa_ref,
                        qkv_out, gates_out, kv1_out, kv2_out, kv3_out, pool_out,
                        lvl, *, tm):
    t = pl.program_id(1)
    last = pl.num_programs(1) - 1
    nlev = len(POOL_WINDOWS)
    top = POOL_PAD + POOL_HIST

    @pl.when(t == 0)
    def _():
        lvl[:, 0:top, :] = jnp.zeros((nlev, top, POOL_WIDTH), F32)

    u = _rms_norm_bf16(x_ref[0], ln1_ref[...])

    def store(g, q, k, v):
        for hp in range(2):
            cols = slice(hp * LANES, (hp + 1) * LANES)
            qkv_out[0, Q_SLAB + 2 * g + hp] = q[:, cols]
            qkv_out[0, K_SLAB + 2 * g + hp] = k[:, cols]
            qkv_out[0, V_SLAB + 2 * g + hp] = v[:, cols]
        if g == 2:
            kv3_out[0, 0:GROUP_WIDTH, :] = k.T
            kv3_out[0, GROUP_WIDTH:, :] = v.T

    _project_qkv(u, w_ref, qn_ref, kn_ref, store)

    lvl[0, top:top + tm, :] = _dot(u, w_ref[:, COL_A:COL_A + POOL_WIDTH])
    pos = t * tm + lax.broadcasted_iota(jnp.int32, (tm, 1), 0)
    diffs = []
    for j, w in enumerate(POOL_WINDOWS):
        assert w == 2 << j and w // 2 <= POOL_PAD
        c0 = j * POOL_GROUP_DIM
        rows = slice(POOL_PAD, top + tm)
        prev = slice(POOL_PAD - w // 2, top + tm - w // 2)
        acc = lvl[j, rows, c0:] + lvl[j, prev, c0:]
        if j + 1 < nlev:
            lvl[j + 1, rows, c0 + POOL_GROUP_DIM:] = acc[:, POOL_GROUP_DIM:]
        cnt = jnp.minimum(pos + 1, w).astype(F32)
        diffs.append(acc[POOL_HIST:, 0:POOL_GROUP_DIM] / cnt - lvl[0, top:top + tm, c0:c0 + POOL_GROUP_DIM])
    diff = jnp.concatenate(diffs, axis=1)

    _pool_branch_and_gates(u, diff, w_ref, lin_ref, scale_ref, wpa_ref, gates_out.at[0])

    @pl.when(t == last)
    def _():
        def pair_rows(slab0, g, r0):
            return jnp.concatenate([qkv_out[0, slab0 + 2 * g, r0:tm, :], qkv_out[0, slab0 + 2 * g + 1, r0:tm, :]], axis=1)
        kv2_out[0, 0:GROUP_WIDTH, :] = pair_rows(K_SLAB, 1, 0).T
        kv2_out[0, GROUP_WIDTH:, :] = pair_rows(V_SLAB, 1, 0).T
        kv1_out[0, 0:GROUP_WIDTH, :] = pair_rows(K_SLAB, 0, tm - ATTN_PATTERNS[0][0]).T
        kv1_out[0, GROUP_WIDTH:, :] = pair_rows(V_SLAB, 0, tm - ATTN_PATTERNS[0][0]).T
        pool_out[0] = lvl[0, top + tm - POOL_STATE:top + tm, :]

    lvl[0, POOL_PAD:top, :] = lvl[0, tm + POOL_PAD:tm + top, :]


def _const_spec(shape):
    nd = len(shape)
    return pl.BlockSpec(shape, lambda *_: (0,) * nd, pipeline_mode=pl.Buffered(1))


def _project_prompt(x, ln1, w_in, qn, kn, lin2, scale, w_pa, *, tm=512):
    b, s, _ = x.shape
    nt = s // tm
    assert s % tm == 0 and tm == ATTN_PATTERNS[1][0] and s == ATTN_PATTERNS[2][0]
    row = lambda width: pl.BlockSpec((1, tm, width), lambda i, j: (i, j, 0))
    tail = lambda rows, width: pl.BlockSpec((1, rows, width), lambda i, j: (i, 0, 0))
    slabs = pl.BlockSpec((1, 3 * N_PAIRS, tm, LANES), lambda i, j: (i, 0, j, 0))
    out_shape = (
        jax.ShapeDtypeStruct((b, 3 * N_PAIRS, s, LANES), F32),
        jax.ShapeDtypeStruct((b, s, 2 * D_MODEL), BF16),
        jax.ShapeDtypeStruct((b, KV_ROWS, ATTN_PATTERNS[0][0]), F32),
        jax.ShapeDtypeStruct((b, KV_ROWS, ATTN_PATTERNS[1][0]), F32),
        jax.ShapeDtypeStruct((b, KV_ROWS, s), F32),
        jax.ShapeDtypeStruct((b, POOL_STATE, POOL_WIDTH), F32),
    )
    return pl.pallas_call(
        functools.partial(_proj_prompt_kernel, tm=tm),
        grid=(b, nt),
        in_specs=[row(D_MODEL), _const_spec(ln1.shape), _const_spec(w_in.shape), _const_spec(qn.shape),
                  _const_spec(kn.shape), _const_spec(lin2.shape), _const_spec(scale.shape),
                  _const_spec(w_pa.shape)],
        out_specs=(slabs, row(2 * D_MODEL),
                   tail(KV_ROWS, ATTN_PATTERNS[0][0]), tail(KV_ROWS, ATTN_PATTERNS[1][0]),
                   pl.BlockSpec((1, KV_ROWS, tm), lambda i, j: (i, 0, j)),
                   tail(POOL_STATE, POOL_WIDTH)),
        out_shape=out_shape,
        scratch_shapes=[pltpu.VMEM((len(POOL_WINDOWS), POOL_PAD + POOL_HIST + tm, POOL_WIDTH), F32)],
        compiler_params=pltpu.CompilerParams(dimension_semantics=("arbitrary", "arbitrary"),
                                             vmem_limit_bytes=VMEM_LIMIT),
        name="project_prompt",
    )(x, ln1, w_in, qn, kn, lin2, scale, w_pa)


def _proj_sample_kernel(x_ref, st_ref, ln1_ref, w_ref, qn_ref, kn_ref, lin_ref, scale_ref, wpa_ref,
                        qkv_out, gates_out, pool_out, abuf, dbuf, *, nb, nt):
    u = _rms_norm_bf16(x_ref[...], ln1_ref[...])

    def store(g, q, k, v):
        c0 = g * GROUP_WIDTH
        qkv_out[:, c0:c0 + GROUP_WIDTH] = q
        qkv_out[:, QKV_WIDTH + c0:QKV_WIDTH + c0 + GROUP_WIDTH] = k
        qkv_out[:, 2 * QKV_WIDTH + c0:2 * QKV_WIDTH + c0 + GROUP_WIDTH] = v

    _project_qkv(u, w_ref, qn_ref, kn_ref, store)

    a = _dot(u, w_ref[:, COL_A:COL_A + POOL_WIDTH])
    for g in range(len(POOL_WINDOWS)):
        abuf[g] = a[:, g * POOL_GROUP_DIM:(g + 1) * POOL_GROUP_DIM]

    def slab(j, g):
        if j < POOL_STATE:
            return st_ref[j, :, g * POOL_GROUP_DIM:(g + 1) * POOL_GROUP_DIM]
        return abuf[g, pl.ds(j - POOL_STATE, nb, stride=nt), :]

    for t in range(nt):
        for g, w in enumerate(POOL_WINDOWS):
            a_t = slab(POOL_STATE + t, g)
            acc = a_t
            for s in range(1, w):
                acc = acc + slab(POOL_STATE + t - s, g)
            cnt = float(min(PAST_LEN + t + 1, w))
            dbuf[g, pl.ds(t, nb, stride=nt), :] = acc / cnt - a_t

    for j in range(POOL_STATE):
        pool_out[j] = jnp.concatenate([slab(j + nt, g) for g in range(len(POOL_WINDOWS))], axis=1)

    diff = jnp.concatenate([dbuf[g] for g in range(len(POOL_WINDOWS))], axis=1)
    _pool_branch_and_gates(u, diff, w_ref, lin_ref, scale_ref, wpa_ref, gates_out)


def _project_sample(x2d, state_t, ln1, w_in, qn, kn, lin2, scale, w_pa, *, nb, nt):
    m = nb * nt
    assert nt <= POOL_STATE and x2d.shape == (m, D_MODEL) and state_t.shape == (POOL_STATE, nb, POOL_WIDTH)
    args = (x2d, state_t, ln1, w_in, qn, kn, lin2, scale, w_pa)
    out_shape = (
        jax.ShapeDtypeStruct((m, 3 * QKV_WIDTH), F32),
        jax.ShapeDtypeStruct((m, 2 * D_MODEL), BF16),
        jax.ShapeDtypeStruct((POOL_STATE, nb, POOL_WIDTH), F32),
    )
    group_buf = pltpu.VMEM((len(POOL_WINDOWS), m, POOL_GROUP_DIM), F32)
    return pl.pallas_call(
        functools.partial(_proj_sample_kernel, nb=nb, nt=nt),
        grid=(1,),
        in_specs=[_const_spec(a.shape) for a in args],
        out_specs=tuple(pl.BlockSpec(o.shape, lambda i, nd=len(o.shape): (0,) * nd) for o in out_shape),
        out_shape=out_shape,
        scratch_shapes=[group_buf, group_buf],
        compiler_params=pltpu.CompilerParams(dimension_semantics=("arbitrary",), vmem_limit_bytes=VMEM_LIMIT),
        name="project_sample",
    )(*args)


def _rows(start, n, stride):
    return pl.ds(start, n) if stride == 1 else pl.ds(start, n, stride=stride)


def _band_bias(nk, offset):
    qi = lax.broadcasted_iota(jnp.int32, (2 * QBLOCK, nk), 0) & (QBLOCK - 1)
    kj = lax.broadcasted_iota(jnp.int32, (2 * QBLOCK, nk), 1)
    dist = qi + offset - kj
    return jnp.where((dist >= 0) & (dist <= BAND), 0.0, NEG).astype(F32)


def _attn_prompt_kernel(qkv_ref, o_ref, o1_buf, l1_buf, o2_buf, l2_buf, s_buf, bias_buf, *, seq):
    lane = lax.broadcasted_iota(jnp.int32, (QBLOCK, LANES), 1)
    first = lane < HEAD_DIM
    bias_buf[0] = _band_bias(2 * QBLOCK, 0)
    bias_buf[1] = _band_bias(2 * QBLOCK, QBLOCK)

    def scores(slot, hp, qstart, kstart, nk, dil, pair0, bias_sel):
        qrows, krows = _rows(qstart, QBLOCK, dil), _rows(kstart, nk, dil)
        q16 = qkv_ref[0, Q_SLAB + pair0 + hp, qrows, :].astype(BF16)
        zero = jnp.zeros_like(q16)
        q2 = jnp.concatenate([jnp.where(first, q16, zero), jnp.where(first, zero, q16)], axis=0)
        s = _dot_nt(q2, qkv_ref[0, K_SLAB + pair0 + hp, krows, :].astype(BF16))
        s_buf[slot, hp, :, 0:nk] = s + bias_buf[bias_sel, :, 0:nk]

    def values(slot, hp, qstart, kstart, nk, dil, pair0, emit):
        qrows, krows = _rows(qstart, QBLOCK, dil), _rows(kstart, nk, dil)
        s = s_buf[slot, hp, :, 0:nk]
        m = jnp.max(s, axis=-1, keepdims=True)
        p = jnp.exp2(s - m).astype(BF16)
        v_ext = jnp.concatenate([qkv_ref[0, V_SLAB + pair0 + hp, krows, :].astype(BF16), jnp.ones((nk, LANES), BF16)], axis=1)
        o = _dot(p, v_ext)
        num = jnp.where(first, o[:QBLOCK, :LANES], o[QBLOCK:, :LANES])
        den = jnp.where(first, o[:QBLOCK, LANES:], o[QBLOCK:, LANES:])
        lse = jnp.where(first, m[:QBLOCK], m[QBLOCK:]) + jnp.log2(den)
        emit(qrows, hp, num / den, lse)

    def run(tasks, nk, dil, pair0, emit):
        assert len(tasks) % 4 == 0

        def stage(score_slot, i_score, value_slot, i_value):
            for d in range(2):
                for hp in range(2):
                    if i_score + d < len(tasks):
                        scores(score_slot + d, hp, *tasks[i_score + d][:2], nk, dil, pair0, tasks[i_score + d][2])
                    values(value_slot + d, hp, *tasks[i_value + d][:2], nk, dil, pair0, emit)

        for d in range(2):
            for hp in range(2):
                scores(d, hp, *tasks[d][:2], nk, dil, pair0, tasks[d][2])
        for i0 in range(0, len(tasks), 4):
            stage(2, i0 + 2, 0, i0)
            stage(0, i0 + 4, 2, i0 + 2)

    def store_to(o_buf, l_buf):
        def emit(qrows, hp, out, lse):
            o_buf[hp, qrows, :] = out
            l_buf[hp, qrows, :] = lse
        return emit

    def subsequence_tasks(dil):
        span = dil * QBLOCK
        return [(r + nb * span, r + max(nb - 1, 0) * span, min(nb, 1)) for r in range(dil) for nb in range(seq // span)]

    assert seq == 16 * QBLOCK
    run(subsequence_tasks(16), QBLOCK, 16, 4, store_to(o2_buf, l2_buf))

    run(subsequence_tasks(4), 2 * QBLOCK, 4, 2, store_to(o1_buf, l1_buf))

    def emit(qrows, hp, out0, lse0):
        lse1, lse2 = l1_buf[hp, qrows, :], l2_buf[hp, qrows, :]
        m = jnp.maximum(jnp.maximum(lse0, lse1), lse2)
        w0, w1, w2 = jnp.exp2(lse0 - m), jnp.exp2(lse1 - m), jnp.exp2(lse2 - m)
        num = w0 * out0 + w1 * o1_buf[hp, qrows, :] + w2 * o2_buf[hp, qrows, :]
        o_ref[0, qrows, hp * LANES:(hp + 1) * LANES] = (num / (w0 + w1 + w2)).astype(o_ref.dtype)

    run(subsequence_tasks(1), 2 * QBLOCK, 1, 0, emit)


def _attention_prompt(qkv):
    b, _, s, _ = qkv.shape
    spec = pl.BlockSpec((1, 3 * N_PAIRS, s, LANES), lambda i: (i, 0, 0, 0))
    group_buf = pltpu.VMEM((2, s, LANES), F32)
    return pl.pallas_call(
        functools.partial(_attn_prompt_kernel, seq=s),
        grid=(b,),
        in_specs=[spec],
        out_specs=pl.BlockSpec((1, s, GROUP_WIDTH), lambda i: (i, 0, 0)),
        out_shape=jax.ShapeDtypeStruct((b, s, GROUP_WIDTH), BF16),
        scratch_shapes=[group_buf, group_buf, group_buf, group_buf,
                        pltpu.VMEM((4, 2, 2 * QBLOCK, 2 * QBLOCK), F32),
                        pltpu.VMEM((2, 2 * QBLOCK, 2 * QBLOCK), F32)],
        compiler_params=pltpu.CompilerParams(dimension_semantics=("arbitrary",), vmem_limit_bytes=VMEM_LIMIT),
        name="attention_prompt",
    )(qkv)


def _attn_sample_parts(qkv_ref, c1_ref, c2_ref, c3_ref, o_ref, n1_ref, n2_ref, n3_ref, *, nt):
    nrow = HEADS_PER_GROUP * nt
    nt_shift = nt.bit_length() - 1
    assert nt == 1 << nt_shift
    row_head = lax.broadcasted_iota(jnp.int32, (nrow, GROUP_WIDTH), 0) >> nt_shift
    col_head = lax.broadcasted_iota(jnp.int32, (nrow, GROUP_WIDTH), 1) >> HEAD_SHIFT
    own = row_head == col_head
    col_head_t = lax.broadcasted_iota(jnp.int32, (nt, GROUP_WIDTH), 1) >> HEAD_SHIFT
    tail_lane = lax.broadcasted_iota(jnp.int32, (KV_ROWS, LANES), 1) >= LANES - nt

    def own_blocks(x):
        acc = jnp.zeros((nt, GROUP_WIDTH), F32)
        for h in range(HEADS_PER_GROUP):
            acc = acc + jnp.where(col_head_t == h, x[h * nt:(h + 1) * nt], 0.0)
        return acc

    def group(g):
        c_ref, n_ref = ((c1_ref, n1_ref), (c2_ref, n2_ref), (c3_ref, n3_ref))[g]
        win, dil = ATTN_PATTERNS[g]
        assert c_ref.shape == (1, KV_ROWS, win) and dil & (dil - 1) == 0
        cs = g * GROUP_WIDTH
        k_new = qkv_ref[:, QKV_WIDTH + cs:QKV_WIDTH + cs + GROUP_WIDTH]
        v_new = qkv_ref[:, 2 * QKV_WIDTH + cs:2 * QKV_WIDTH + cs + GROUP_WIDTH]

        new_t = jnp.concatenate([jnp.zeros((LANES - nt, KV_ROWS), F32), jnp.concatenate([k_new, v_new], axis=1)],
                                axis=0).T
        ntiles = win // LANES
        rot = pltpu.roll(c_ref[0, :, 0:LANES], LANES - nt, axis=1)
        for j in range(ntiles):
            nxt = pltpu.roll(c_ref[0, :, (j + 1) * LANES:(j + 2) * LANES], LANES - nt, axis=1) if j + 1 < ntiles else new_t
            n_ref[0, :, j * LANES:(j + 1) * LANES] = jnp.where(tail_lane, nxt, rot)
            rot = nxt

        q = qkv_ref[:, cs:cs + GROUP_WIDTH]
        qbd = jnp.where(own, jnp.concatenate([q] * HEADS_PER_GROUP, axis=0), 0.0).astype(BF16)
        k_old = c_ref[0, 0:GROUP_WIDTH, :].astype(BF16)
        v_old = c_ref[0, GROUP_WIDTH:, :].astype(BF16)
        t_old = lax.broadcasted_iota(jnp.int32, (nrow, win), 0) & (nt - 1)
        i_old = lax.broadcasted_iota(jnp.int32, (nrow, win), 1)
        ok_old = (i_old >= t_old) & (((i_old - t_old) & (dil - 1)) == 0)
        t_new = lax.broadcasted_iota(jnp.int32, (nrow, nt), 0) & (nt - 1)
        i_new = lax.broadcasted_iota(jnp.int32, (nrow, nt), 1)
        ok_new = (i_new <= t_new) & (((t_new - i_new) & (dil - 1)) == 0)
        s_old = _dot(qbd, k_old) + jnp.where(ok_old, 0.0, NEG)
        s_new = _dot_nt(qbd, k_new.astype(BF16)) + jnp.where(ok_new, 0.0, NEG)
        m = jnp.maximum(jnp.max(s_old, axis=-1, keepdims=True), jnp.max(s_new, axis=-1, keepdims=True))
        p_old = jnp.exp2(s_old - m)
        p_new = jnp.exp2(s_new - m)
        den = jnp.sum(p_old, axis=-1, keepdims=True) + jnp.sum(p_new, axis=-1, keepdims=True)
        o = (_dot_nt(p_old.astype(BF16), v_old) + _dot(p_new.astype(BF16), v_new.astype(BF16))) / den
        return own_blocks(o), own_blocks(jnp.broadcast_to(m + jnp.log2(den), (nrow, GROUP_WIDTH)))

    def combine(parts):
        (o0, l0), (o1, l1), (o2, l2) = parts
        m = jnp.maximum(jnp.maximum(l0, l1), l2)
        w0, w1, w2 = jnp.exp2(l0 - m), jnp.exp2(l1 - m), jnp.exp2(l2 - m)
        o_ref[...] = (w0 * o0 + w1 * o1 + w2 * o2) / (w0 + w1 + w2)

    return group, combine


N_MERGE_IN = 8
N_SAMPLE_IN = 4


def _merge_mlp_kernel(*refs, ff_chunk, nt_sample):
    x_ref, gates_ref, at_ref, wpb_ref, wo_ref, ln2_ref, wup_ref, wdn_ref = refs[:N_MERGE_IN]
    y_ref = refs[N_MERGE_IN + (N_SAMPLE_IN if nt_sample else 0)]
    pending = []
    if nt_sample:
        group, combine = _attn_sample_parts(*refs[N_MERGE_IN:N_MERGE_IN + N_SAMPLE_IN],
                                            *refs[N_MERGE_IN + N_SAMPLE_IN + 1:], nt=nt_sample)
        parts = [None] * len(ATTN_PATTERNS)

        def run_group(g):
            parts[g] = group(g)

        def run_combine():
            combine(parts)
        pending = [functools.partial(run_group, g) for g in (2, 1, 0)] + [run_combine]

    def issue_sample_piece():
        if pending:
            pending.pop(0)()

    issue_sample_piece()
    branch_b = _dot(at_ref[...].astype(BF16), wpb_ref[...])
    mixed = (gates_ref[:, 0:D_MODEL].astype(F32) + gates_ref[:, D_MODEL:].astype(F32) * branch_b).astype(BF16)
    h = x_ref[...] + _dot(mixed, wo_ref[...])
    u = _rms_norm_bf16(h, ln2_ref[...])
    acc = h
    for c in range(0, D_FF, ff_chunk):
        z = jnp.maximum(_dot(u, wup_ref[:, c:c + ff_chunk]), 0.0)
        acc = acc + _dot((z * z).astype(BF16), wdn_ref[c:c + ff_chunk, :])
        issue_sample_piece()
    y_ref[...] = acc
    while pending:
        issue_sample_piece()


def _merge_mlp(x2d, gates, attn, w_pb, w_o, ln2, w_up, w_down, sample=None, *, tm=512, ff_chunk=1024):
    m = x2d.shape[0]
    tm = min(tm, m)
    assert m % tm == 0
    row = lambda width: pl.BlockSpec((tm, width), lambda i: (i, 0))
    args = [x2d, gates, attn, w_pb, w_o, ln2, w_up, w_down]
    in_specs = [row(D_MODEL), row(2 * D_MODEL), row(GROUP_WIDTH)] + [_const_spec(w.shape) for w in args[3:]]
    out_specs = [row(D_MODEL)]
    out_shape = [jax.ShapeDtypeStruct((m, D_MODEL), F32)]
    nt = 0
    if sample is not None:
        *sample_args, nt = sample
        qkv, c1, c2, c3 = sample_args
        nb = c1.shape[0]
        assert m // tm == nb and qkv.shape == (nb * nt, 3 * QKV_WIDTH), "one sample batch element per token tile"
        tok = pl.BlockSpec((nt, 3 * QKV_WIDTH), lambda i: (i, 0))
        cache = lambda c: pl.BlockSpec((1,) + c.shape[1:], lambda i: (i, 0, 0))
        args += sample_args
        in_specs += [tok, cache(c1), cache(c2), cache(c3)]
        out_specs += [pl.BlockSpec((nt, GROUP_WIDTH), lambda i: (i, 0)), cache(c1), cache(c2), cache(c3)]
        out_shape += [jax.ShapeDtypeStruct((nb * nt, GROUP_WIDTH), F32)] + [jax.ShapeDtypeStruct(c.shape, F32) for c in (c1, c2, c3)]
    out = pl.pallas_call(
        functools.partial(_merge_mlp_kernel, ff_chunk=ff_chunk, nt_sample=nt),
        grid=(m // tm,),
        in_specs=in_specs,
        out_specs=tuple(out_specs),
        out_shape=tuple(out_shape),
        compiler_params=pltpu.CompilerParams(dimension_semantics=("arbitrary",),
                                             vmem_limit_bytes=VMEM_LIMIT_FUSED if sample is not None else VMEM_LIMIT),
        name="merge_mlp",
    )(*args)
    return out if sample is not None else out[0]


def _position_minor(c):
    n, w = c.shape[1], c.shape[2]
    return jnp.transpose(c[0], (0, 2, 3, 4, 1)).reshape(n, KV_ROWS, w)


def _position_major(c):
    n, _, w = c.shape
    return jnp.transpose(c.reshape(n, 2, HEADS_PER_GROUP, HEAD_DIM, w), (0, 4, 1, 2, 3))[None]


def kernel(x_prompt, x_sample, state_pool, cache_kv1, cache_kv2, cache_kv3, ln1, w_in, q_norm, k_norm,
           pool_lin, pool_scale, w_pa, w_pb, w_o, ln2, w_up, w_down):
    depth = ln1.shape[0]
    assert depth == 1
    b, s, _ = x_prompt.shape
    nb, nt, _ = x_sample.shape

    w_in16, w_pa16, w_pb16 = w_in[0].astype(BF16), w_pa[0].astype(BF16), w_pb[0].astype(BF16)
    w_o16, w_up16, w_dn16 = w_o[0].astype(BF16), w_up[0].astype(BF16), w_down[0].astype(BF16)
    ln1v, ln2v = ln1[0].reshape(1, D_MODEL), ln2[0].reshape(1, D_MODEL)
    qn, kn = q_norm[0].reshape(1, QKV_WIDTH), k_norm[0].reshape(1, QKV_WIDTH)
    scale = pool_scale[0].reshape(1, POOL_WIDTH)
    lin = pool_lin[0].astype(BF16)
    zeros = jnp.zeros((POOL_GROUP_DIM, POOL_GROUP_DIM), BF16)
    lin2 = jnp.stack([jnp.block([[lin[0], zeros], [zeros, lin[1]]]), jnp.block([[lin[2], zeros], [zeros, lin[3]]])])

    qkv, gates, kv1, kv2, kv3, pool_p = _project_prompt(x_prompt, ln1v, w_in16, qn, kn, lin2, scale, w_pa16)
    xs = x_sample.reshape(nb * nt, D_MODEL)
    state_t = jnp.transpose(state_pool[0], (1, 0, 2))
    qkv_s, gates_s, pool_s = _project_sample(xs, state_t, ln1v, w_in16, qn, kn, lin2, scale, w_pa16, nb=nb, nt=nt)

    attn = _attention_prompt(qkv)
    caches = [_position_minor(c) for c in (cache_kv1, cache_kv2, cache_kv3)]
    y_prompt, attn_s, n1, n2, n3 = _merge_mlp(
        x_prompt.reshape(b * s, D_MODEL), gates.reshape(b * s, 2 * D_MODEL), attn.reshape(b * s, GROUP_WIDTH),
        w_pb16, w_o16, ln2v, w_up16, w_dn16, sample=(qkv_s, *caches, nt))
    y_prompt = y_prompt.reshape(b, s, D_MODEL)
    y_sample = _merge_mlp(xs, gates_s, attn_s, w_pb16, w_o16, ln2v, w_up16, w_dn16).reshape(nb, nt, D_MODEL)

    return (y_prompt, y_sample,
            pool_p.reshape(1, b, POOL_STATE, POOL_WIDTH),
            _position_major(kv1), _position_major(kv2), _position_major(kv3),
            jnp.transpose(pool_s, (1, 0, 2))[None],
            _position_major(n1), _position_major(n2), _position_major(n3))
```

```python
import functools

import jax
import jax.numpy as jnp
from jax import lax
from jax.experimental import pallas as pl
from jax.experimental.pallas import tpu as pltpu

F32 = jnp.float32
BF16 = jnp.bfloat16

D_MODEL = 1024
POOL_WINDOWS = (2, 4, 8, 16)
POOL_WIDTH = 512
POOL_GROUP_DIM = 128
POOL_STATE = 15
POOL_HIST = 16
POOL_PAD = 8
ATTN_PATTERNS = ((128, 1), (512, 4), (2048, 16))
BAND = 128
HEAD_DIM = 64
HEAD_SHIFT = 6
HEADS_PER_GROUP = 4
GROUP_WIDTH = HEADS_PER_GROUP * HEAD_DIM
KV_ROWS = 2 * GROUP_WIDTH
QKV_WIDTH = 3 * GROUP_WIDTH
N_PAIRS = QKV_WIDTH // 128
Q_SLAB, K_SLAB, V_SLAB = 0, N_PAIRS, 2 * N_PAIRS
D_FF = 4096
PAST_LEN = 8192
EPS = 1e-6
COL_A, COL_Q, COL_K, COL_V, COL_GA, COL_GB = 0, 512, 1280, 2048, 2816, 3840
Q_SCALE = HEAD_DIM ** -0.5 * 1.4426950408889634
QBLOCK = 128
NEG = -1e30
LANES = 128
VMEM_LIMIT = 56 * 1024 * 1024
VMEM_LIMIT_FUSED = 62 * 1024 * 1024


def _dot(a, b):
    return jnp.dot(a, b, preferred_element_type=F32)


def _dot_nt(a, b):
    return lax.dot_general(a, b, (((1,), (1,)), ((), ())), preferred_element_type=F32)


def _rms_norm_bf16(x, w):
    ms = jnp.mean(x * x, axis=-1, keepdims=True)
    return ((x * lax.rsqrt(ms + EPS)) * w).astype(BF16)


def _sigmoid(x):
    return 1.0 / (1.0 + jnp.exp(-x))


def _head_norm(y, w):
    first = lax.broadcasted_iota(jnp.int32, (1, LANES), 1) < HEAD_DIM
    outs = []
    for g in range(3):
        halves = []
        for hp in range(2):
            cols = slice(g * GROUP_WIDTH + hp * LANES, g * GROUP_WIDTH + (hp + 1) * LANES)
            yg = y[:, cols]
            sq = yg * yg
            s0 = jnp.sum(jnp.where(first, sq, 0.0), axis=-1, keepdims=True)
            s1 = jnp.sum(jnp.where(first, 0.0, sq), axis=-1, keepdims=True)
            ms = jnp.where(first, s0, s1) * (1.0 / HEAD_DIM)
            halves.append((yg * lax.rsqrt(ms + EPS)) * w[:, cols])
        outs.append(jnp.concatenate(halves, axis=1))
    return outs


def _project_qkv(u, w_ref, qn_ref, kn_ref, store):
    q = _head_norm(_dot(u, w_ref[:, COL_Q:COL_Q + QKV_WIDTH]), qn_ref[...] * Q_SCALE)
    k = _head_norm(_dot(u, w_ref[:, COL_K:COL_K + QKV_WIDTH]), kn_ref[...])
    v = _dot(u, w_ref[:, COL_V:COL_V + QKV_WIDTH])
    for g in range(3):
        store(g, q[g], k[g], v[:, g * GROUP_WIDTH:(g + 1) * GROUP_WIDTH])


def _pool_branch_and_gates(u, diff, w_ref, lin_ref, scale_ref, wpa_ref, gates_out):
    d16 = diff.astype(BF16)
    half = POOL_WIDTH // 2
    z = jnp.concatenate([_dot(d16[:, 0:half], lin_ref[0]), _dot(d16[:, half:], lin_ref[1])], axis=1)
    a_mix = (z * scale_ref[...]).astype(BF16)
    branch_a = _dot(a_mix, wpa_ref[...])
    g_a = _dot(u, w_ref[:, COL_GA:COL_GA + D_MODEL])
    gates_out[:, 0:D_MODEL] = (_sigmoid(g_a) * branch_a).astype(BF16)
    g_b = _dot(u, w_ref[:, COL_GB:COL_GB + D_MODEL])
    gates_out[:, D_MODEL:] = _sigmoid(g_b).astype(BF16)


def _proj_prompt_kernel(x_ref, ln1_ref, w_ref, qn_ref, kn_ref, lin_ref, scale_ref, wpa_ref,
                        qkv_out, gates_out, kv1_out, kv2_out, kv3_out, pool_out,
                        lvl, *, tm):
    t = pl.program_id(1)
    last = pl.num_programs(1) - 1
    nlev = len(POOL_WINDOWS)
    top = POOL_PAD + POOL_HIST

    @pl.when(t == 0)
    def _():
        lvl[:, 0:top, :] = jnp.zeros((nlev, top, POOL_WIDTH), F32)

    u = _rms_norm_bf16(x_ref[0], ln1_ref[...])

    def store(g, q, k, v):
        for hp in range(2):
            cols = slice(hp * LANES, (hp + 1) * LANES)
            qkv_out[0, Q_SLAB + 2 * g + hp] = q[:, cols]
            qkv_out[0, K_SLAB + 2 * g + hp] = k[:, cols]
            qkv_out[0, V_SLAB + 2 * g + hp] = v[:, cols]
        if g == 2:
            kv3_out[0, 0:GROUP_WIDTH, :] = k.T
            kv3_out[0, GROUP_WIDTH:, :] = v.T

    _project_qkv(u, w_ref, qn_ref, kn_ref, store)

    lvl[0, top:top + tm, :] = _dot(u, w_ref[:, COL_A:COL_A + POOL_WIDTH])
    pos = t * tm + lax.broadcasted_iota(jnp.int32, (tm, 1), 0)
    diffs = []
    for j, w in enumerate(POOL_WINDOWS):
        assert w == 2 << j and w // 2 <= POOL_PAD
        c0 = j * POOL_GROUP_DIM
        rows = slice(POOL_PAD, top + tm)
        prev = slice(POOL_PAD - w // 2, top + tm - w // 2)
        acc = lvl[j, rows, c0:] + lvl[j, prev, c0:]
        if j + 1 < nlev:
            lvl[j + 1, rows, c0 + POOL_GROUP_DIM:] = acc[:, POOL_GROUP_DIM:]
        cnt = jnp.minimum(pos + 1, w).astype(F32)
        diffs.append(acc[POOL_HIST:, 0:POOL_GROUP_DIM] / cnt - lvl[0, top:top + tm, c0:c0 + POOL_GROUP_DIM])
    diff = jnp.concatenate(diffs, axis=1)

    _pool_branch_and_gates(u, diff, w_ref, lin_ref, scale_ref, wpa_ref, gates_out.at[0])

    @pl.when(t == last)
    def _():
        def pair_rows(slab0, g, r0):
            return jnp.concatenate([qkv_out[0, slab0 + 2 * g, r0:tm, :], qkv_out[0, slab0 + 2 * g + 1, r0:tm, :]], axis=1)
        kv2_out[0, 0:GROUP_WIDTH, :] = pair_rows(K_SLAB, 1, 0).T
        kv2_out[0, GROUP_WIDTH:, :] = pair_rows(V_SLAB, 1, 0).T
        kv1_out[0, 0:GROUP_WIDTH, :] = pair_rows(K_SLAB, 0, tm - ATTN_PATTERNS[0][0]).T
        kv1_out[0, GROUP_WIDTH:, :] = pair_rows(V_SLAB, 0, tm - ATTN_PATTERNS[0][0]).T
        pool_out[0] = lvl[0, top + tm - POOL_STATE:top + tm, :]

    lvl[0, POOL_PAD:top, :] = lvl[0, tm + POOL_PAD:tm + top, :]


def _const_spec(shape):
    nd = len(shape)
    return pl.BlockSpec(shape, lambda *_: (0,) * nd, pipeline_mode=pl.Buffered(1))


def _project_prompt(x, ln1, w_in, qn, kn, lin2, scale, w_pa, *, tm=512):
    b, s, _ = x.shape
    nt = s // tm
    assert s % tm == 0 and tm == ATTN_PATTERNS[1][0] and s == ATTN_PATTERNS[2][0]
    row = lambda width: pl.BlockSpec((1, tm, width), lambda i, j: (i, j, 0))
    tail = lambda rows, width: pl.BlockSpec((1, rows, width), lambda i, j: (i, 0, 0))
    slabs = pl.BlockSpec((1, 3 * N_PAIRS, tm, LANES), lambda i, j: (i, 0, j, 0))
    out_shape = (
        jax.ShapeDtypeStruct((b, 3 * N_PAIRS, s, LANES), F32),
        jax.ShapeDtypeStruct((b, s, 2 * D_MODEL), BF16),
        jax.ShapeDtypeStruct((b, KV_ROWS, ATTN_PATTERNS[0][0]), F32),
        jax.ShapeDtypeStruct((b, KV_ROWS, ATTN_PATTERNS[1][0]), F32),
        jax.ShapeDtypeStruct((b, KV_ROWS, s), F32),
        jax.ShapeDtypeStruct((b, POOL_STATE, POOL_WIDTH), F32),
    )
    return pl.pallas_call(
        functools.partial(_proj_prompt_kernel, tm=tm),
        grid=(b, nt),
        in_specs=[row(D_MODEL), _const_spec(ln1.shape), _const_spec(w_in.shape), _const_spec(qn.shape),
                  _const_spec(kn.shape), _const_spec(lin2.shape), _const_spec(scale.shape),
                  _const_spec(w_pa.shape)],
        out_specs=(slabs, row(2 * D_MODEL),
                   tail(KV_ROWS, ATTN_PATTERNS[0][0]), tail(KV_ROWS, ATTN_PATTERNS[1][0]),
                   pl.BlockSpec((1, KV_ROWS, tm), lambda i, j: (i, 0, j)),
                   tail(POOL_STATE, POOL_WIDTH)),
        out_shape=out_shape,
        scratch_shapes=[pltpu.VMEM((len(POOL_WINDOWS), POOL_PAD + POOL_HIST + tm, POOL_WIDTH), F32)],
        compiler_params=pltpu.CompilerParams(dimension_semantics=("arbitrary", "arbitrary"),
                                             vmem_limit_bytes=VMEM_LIMIT),
        name="project_prompt",
    )(x, ln1, w_in, qn, kn, lin2, scale, w_pa)


def _proj_sample_kernel(x_ref, st_ref, ln1_ref, w_ref, qn_ref, kn_ref, lin_ref, scale_ref, wpa_ref,
                        qkv_out, gates_out, pool_out, abuf, dbuf, *, nb, nt):
    u = _rms_norm_bf16(x_ref[...], ln1_ref[...])

    def store(g, q, k, v):
        c0 = g * GROUP_WIDTH
        qkv_out[:, c0:c0 + GROUP_WIDTH] = q
        qkv_out[:, QKV_WIDTH + c0:QKV_WIDTH + c0 + GROUP_WIDTH] = k
        qkv_out[:, 2 * QKV_WIDTH + c0:2 * QKV_WIDTH + c0 + GROUP_WIDTH] = v

    _project_qkv(u, w_ref, qn_ref, kn_ref, store)

    a = _dot(u, w_ref[:, COL_A:COL_A + POOL_WIDTH])
    for g in range(len(POOL_WINDOWS)):
        abuf[g] = a[:, g * POOL_GROUP_DIM:(g + 1) * POOL_GROUP_DIM]

    def slab(j, g):
        if j < POOL_STATE:
            return st_ref[j, :, g * POOL_GROUP_DIM:(g + 1) * POOL_GROUP_DIM]
        return abuf[g, pl.ds(j - POOL_STATE, nb, stride=nt), :]

    for t in range(nt):
        for g, w in enumerate(POOL_WINDOWS):
            a_t = slab(POOL_STATE + t, g)
            acc = a_t
            for s in range(1, w):
                acc = acc + slab(POOL_STATE + t - s, g)
            cnt = float(min(PAST_LEN + t + 1, w))
            dbuf[g, pl.ds(t, nb, stride=nt), :] = acc / cnt - a_t

    for j in range(POOL_STATE):
        pool_out[j] = jnp.concatenate([slab(j + nt, g) for g in range(len(POOL_WINDOWS))], axis=1)

    diff = jnp.concatenate([dbuf[g] for g in range(len(POOL_WINDOWS))], axis=1)
    _pool_branch_and_gates(u, diff, w_ref, lin_ref, scale_ref, wpa_ref, gates_out)


def _project_sample(x2d, state_t, ln1, w_in, qn, kn, lin2, scale, w_pa, *, nb, nt):
    m = nb * nt
    assert nt <= POOL_STATE and x2d.shape == (m, D_MODEL) and state_t.shape == (POOL_STATE, nb, POOL_WIDTH)
    args = (x2d, state_t, ln1, w_in, qn, kn, lin2, scale, w_pa)
    out_shape = (
        jax.ShapeDtypeStruct((m, 3 * QKV_WIDTH), F32),
        jax.ShapeDtypeStruct((m, 2 * D_MODEL), BF16),
        jax.ShapeDtypeStruct((POOL_STATE, nb, POOL_WIDTH), F32),
    )
    group_buf = pltpu.VMEM((len(POOL_WINDOWS), m, POOL_GROUP_DIM), F32)
    return pl.pallas_call(
        functools.partial(_proj_sample_kernel, nb=nb, nt=nt),
        grid=(1,),
        in_specs=[_const_spec(a.shape) for a in args],
        out_specs=tuple(pl.BlockSpec(o.shape, lambda i, nd=len(o.shape): (0,) * nd) for o in out_shape),
        out_shape=out_shape,
        scratch_shapes=[group_buf, group_buf],
        compiler_params=pltpu.CompilerParams(dimension_semantics=("arbitrary",), vmem_limit_bytes=VMEM_LIMIT),
        name="project_sample",
    )(*args)


def _rows(start, n, stride):
    return pl.ds(start, n) if stride == 1 else pl.ds(start, n, stride=stride)


def _band_bias(nk, offset):
    qi = lax.broadcasted_iota(jnp.int32, (2 * QBLOCK, nk), 0) & (QBLOCK - 1)
    kj = lax.broadcasted_iota(jnp.int32, (2 * QBLOCK, nk), 1)
    dist = qi + offset - kj
    return jnp.where((dist >= 0) & (dist <= BAND), 0.0, NEG).astype(F32)


def _attn_prompt_kernel(qkv_ref, o_ref, o1_buf, l1_buf, o2_buf, l2_buf, s_buf, bias_buf, *, seq):
    lane = lax.broadcasted_iota(jnp.int32, (QBLOCK, LANES), 1)
    first = lane < HEAD_DIM
    bias_buf[0] = _band_bias(2 * QBLOCK, 0)
    bias_buf[1] = _band_bias(2 * QBLOCK, QBLOCK)

    def scores(slot, hp, qstart, kstart, nk, dil, pair0, bias_sel):
        qrows, krows = _rows(qstart, QBLOCK, dil), _rows(kstart, nk, dil)
        q16 = qkv_ref[0, Q_SLAB + pair0 + hp, qrows, :].astype(BF16)
        zero = jnp.zeros_like(q16)
        q2 = jnp.concatenate([jnp.where(first, q16, zero), jnp.where(first, zero, q16)], axis=0)
        s = _dot_nt(q2, qkv_ref[0, K_SLAB + pair0 + hp, krows, :].astype(BF16))
        s_buf[slot, hp, :, 0:nk] = s + bias_buf[bias_sel, :, 0:nk]

    def values(slot, hp, qstart, kstart, nk, dil, pair0, emit):
        qrows, krows = _rows(qstart, QBLOCK, dil), _rows(kstart, nk, dil)
        s = s_buf[slot, hp, :, 0:nk]
        m = jnp.max(s, axis=-1, keepdims=True)
        p = jnp.exp2(s - m).astype(BF16)
        v_ext = jnp.concatenate([qkv_ref[0, V_SLAB + pair0 + hp, krows, :].astype(BF16), jnp.ones((nk, LANES), BF16)], axis=1)
        o = _dot(p, v_ext)
        num = jnp.where(first, o[:QBLOCK, :LANES], o[QBLOCK:, :LANES])
        den = jnp.where(first, o[:QBLOCK, LANES:], o[QBLOCK:, LANES:])
        lse = jnp.where(first, m[:QBLOCK], m[QBLOCK:]) + jnp.log2(den)
        emit(qrows, hp, num / den, lse)

    def run(tasks):
        assert len(tasks) % 4 == 0

        def stage(score_slot, i_score, value_slot, i_value):
            for d in range(2):
                for hp in range(2):
                    if i_score + d < len(tasks):
                        qs, ks, sel, nk, dil, pair0, _ = tasks[i_score + d]
                        scores(score_slot + d, hp, qs, ks, nk, dil, pair0, sel)
                    qs, ks, _, nk, dil, pair0, emit = tasks[i_value + d]
                    values(value_slot + d, hp, qs, ks, nk, dil, pair0, emit)

        for d in range(2):
            for hp in range(2):
                qs, ks, sel, nk, dil, pair0, _ = tasks[d]
                scores(d, hp, qs, ks, nk, dil, pair0, sel)
        for i0 in range(0, len(tasks), 4):
            stage(2, i0 + 2, 0, i0)
            stage(0, i0 + 4, 2, i0 + 2)

    def store_to(o_buf, l_buf):
        def emit(qrows, hp, out, lse):
            o_buf[hp, qrows, :] = out
            l_buf[hp, qrows, :] = lse
        return emit

    def subsequence_tasks(dil, nk, pair0, emit):
        span = dil * QBLOCK
        return [(r + nb * span, r + max(nb - 1, 0) * span, min(nb, 1), nk, dil, pair0, emit)
                for nb in range(seq // span) for r in range(dil)]

    def combine(qrows, hp, out0, lse0):
        lse1, lse2 = l1_buf[hp, qrows, :], l2_buf[hp, qrows, :]
        m = jnp.maximum(jnp.maximum(lse0, lse1), lse2)
        w0, w1, w2 = jnp.exp2(lse0 - m), jnp.exp2(lse1 - m), jnp.exp2(lse2 - m)
        num = w0 * out0 + w1 * o1_buf[hp, qrows, :] + w2 * o2_buf[hp, qrows, :]
        o_ref[0, qrows, hp * LANES:(hp + 1) * LANES] = (num / (w0 + w1 + w2)).astype(o_ref.dtype)

    assert seq == 16 * QBLOCK
    wide = subsequence_tasks(16, QBLOCK, 4, store_to(o2_buf, l2_buf))
    mid = subsequence_tasks(4, 2 * QBLOCK, 2, store_to(o1_buf, l1_buf))
    near = subsequence_tasks(1, 2 * QBLOCK, 0, combine)
    assert len(mid) == len(near) and len(mid) % 4 == 0
    mixed = mid[0:4]
    for i in range(0, len(near), 2):
        mixed += near[i:i + 2] + mid[i + 4:i + 6]
    run(wide + mixed)


def _attention_prompt(qkv):
    b, _, s, _ = qkv.shape
    spec = pl.BlockSpec((1, 3 * N_PAIRS, s, LANES), lambda i: (i, 0, 0, 0))
    group_buf = pltpu.VMEM((2, s, LANES), F32)
    return pl.pallas_call(
        functools.partial(_attn_prompt_kernel, seq=s),
        grid=(b,),
        in_specs=[spec],
        out_specs=pl.BlockSpec((1, s, GROUP_WIDTH), lambda i: (i, 0, 0)),
        out_shape=jax.ShapeDtypeStruct((b, s, GROUP_WIDTH), BF16),
        scratch_shapes=[group_buf, group_buf, group_buf, group_buf,
                        pltpu.VMEM((4, 2, 2 * QBLOCK, 2 * QBLOCK), F32),
                        pltpu.VMEM((2, 2 * QBLOCK, 2 * QBLOCK), F32)],
        compiler_params=pltpu.CompilerParams(dimension_semantics=("arbitrary",), vmem_limit_bytes=VMEM_LIMIT),
        name="attention_prompt",
    )(qkv)


def _attn_sample_parts(qkv_ref, c1_ref, c2_ref, c3_ref, o_ref, n1_ref, n2_ref, n3_ref, *, nt):
    nrow = HEADS_PER_GROUP * nt
    nt_shift = nt.bit_length() - 1
    assert nt == 1 << nt_shift
    row_head = lax.broadcasted_iota(jnp.int32, (nrow, GROUP_WIDTH), 0) >> nt_shift
    col_head = lax.broadcasted_iota(jnp.int32, (nrow, GROUP_WIDTH), 1) >> HEAD_SHIFT
    own = row_head == col_head
    col_head_t = lax.broadcasted_iota(jnp.int32, (nt, GROUP_WIDTH), 1) >> HEAD_SHIFT
    tail_lane = lax.broadcasted_iota(jnp.int32, (KV_ROWS, LANES), 1) >= LANES - nt

    def own_blocks(x):
        acc = jnp.zeros((nt, GROUP_WIDTH), F32)
        for h in range(HEADS_PER_GROUP):
            acc = acc + jnp.where(col_head_t == h, x[h * nt:(h + 1) * nt], 0.0)
        return acc

    def group(g):
        c_ref, n_ref = ((c1_ref, n1_ref), (c2_ref, n2_ref), (c3_ref, n3_ref))[g]
        win, dil = ATTN_PATTERNS[g]
        assert c_ref.shape == (1, KV_ROWS, win) and dil & (dil - 1) == 0
        cs = g * GROUP_WIDTH
        k_new = qkv_ref[:, QKV_WIDTH + cs:QKV_WIDTH + cs + GROUP_WIDTH]
        v_new = qkv_ref[:, 2 * QKV_WIDTH + cs:2 * QKV_WIDTH + cs + GROUP_WIDTH]

        new_t = jnp.concatenate([jnp.zeros((LANES - nt, KV_ROWS), F32), jnp.concatenate([k_new, v_new], axis=1)],
                                axis=0).T
        ntiles = win // LANES
        rot = pltpu.roll(c_ref[0, :, 0:LANES], LANES - nt, axis=1)
        for j in range(ntiles):
            nxt = pltpu.roll(c_ref[0, :, (j + 1) * LANES:(j + 2) * LANES], LANES - nt, axis=1) if j + 1 < ntiles else new_t
            n_ref[0, :, j * LANES:(j + 1) * LANES] = jnp.where(tail_lane, nxt, rot)
            rot = nxt

        q = qkv_ref[:, cs:cs + GROUP_WIDTH]
        qbd = jnp.where(own, jnp.concatenate([q] * HEADS_PER_GROUP, axis=0), 0.0).astype(BF16)
        k_old = c_ref[0, 0:GROUP_WIDTH, :].astype(BF16)
        v_old = c_ref[0, GROUP_WIDTH:, :].astype(BF16)
        t_old = lax.broadcasted_iota(jnp.int32, (nrow, win), 0) & (nt - 1)
        i_old = lax.broadcasted_iota(jnp.int32, (nrow, win), 1)
        ok_old = (i_old >= t_old) & (((i_old - t_old) & (dil - 1)) == 0)
        t_new = lax.broadcasted_iota(jnp.int32, (nrow, nt), 0) & (nt - 1)
        i_new = lax.broadcasted_iota(jnp.int32, (nrow, nt), 1)
        ok_new = (i_new <= t_new) & (((t_new - i_new) & (dil - 1)) == 0)
        s_old = _dot(qbd, k_old) + jnp.where(ok_old, 0.0, NEG)
        s_new = _dot_nt(qbd, k_new.astype(BF16)) + jnp.where(ok_new, 0.0, NEG)
        m = jnp.maximum(jnp.max(s_old, axis=-1, keepdims=True), jnp.max(s_new, axis=-1, keepdims=True))
        p_old = jnp.exp2(s_old - m)
        p_new = jnp.exp2(s_new - m)
        den = jnp.sum(p_old, axis=-1, keepdims=True) + jnp.sum(p_new, axis=-1, keepdims=True)
        o = (_dot_nt(p_old.astype(BF16), v_old) + _dot(p_new.astype(BF16), v_new.astype(BF16))) / den
        return own_blocks(o), own_blocks(jnp.broadcast_to(m + jnp.log2(den), (nrow, GROUP_WIDTH)))

    def combine(parts):
        (o0, l0), (o1, l1), (o2, l2) = parts
        m = jnp.maximum(jnp.maximum(l0, l1), l2)
        w0, w1, w2 = jnp.exp2(l0 - m), jnp.exp2(l1 - m), jnp.exp2(l2 - m)
        o_ref[...] = (w0 * o0 + w1 * o1 + w2 * o2) / (w0 + w1 + w2)

    return group, combine


N_MERGE_IN = 8
N_SAMPLE_IN = 4


def _merge_mlp_kernel(*refs, ff_chunk, nt_sample):
    x_ref, gates_ref, at_ref, wpb_ref, wo_ref, ln2_ref, wup_ref, wdn_ref = refs[:N_MERGE_IN]
    y_ref = refs[N_MERGE_IN + (N_SAMPLE_IN if nt_sample else 0)]
    pending = []
    if nt_sample:
        group, combine = _attn_sample_parts(*refs[N_MERGE_IN:N_MERGE_IN + N_SAMPLE_IN],
                                            *refs[N_MERGE_IN + N_SAMPLE_IN + 1:], nt=nt_sample)
        parts = [None] * len(ATTN_PATTERNS)

        def run_group(g):
            parts[g] = group(g)

        def run_combine():
            combine(parts)
        pending = [functools.partial(run_group, g) for g in (2, 1, 0)] + [run_combine]

    def issue_sample_piece():
        if pending:
            pending.pop(0)()

    issue_sample_piece()
    branch_b = _dot(at_ref[...].astype(BF16), wpb_ref[...])
    mixed = (gates_ref[:, 0:D_MODEL].astype(F32) + gates_ref[:, D_MODEL:].astype(F32) * branch_b).astype(BF16)
    h = x_ref[...] + _dot(mixed, wo_ref[...])
    u = _rms_norm_bf16(h, ln2_ref[...])
    acc = h
    for c in range(0, D_FF, ff_chunk):
        z = jnp.maximum(_dot(u, wup_ref[:, c:c + ff_chunk]), 0.0)
        acc = acc + _dot((z * z).astype(BF16), wdn_ref[c:c + ff_chunk, :])
        issue_sample_piece()
    y_ref[...] = acc
    while pending:
        issue_sample_piece()


def _merge_mlp(x2d, gates, attn, w_pb, w_o, ln2, w_up, w_down, sample=None, *, tm=512, ff_chunk=1024):
    m = x2d.shape[0]
    tm = min(tm, m)
    assert m % tm == 0
    row = lambda width: pl.BlockSpec((tm, width), lambda i: (i, 0))
    args = [x2d, gates, attn, w_pb, w_o, ln2, w_up, w_down]
    in_specs = [row(D_MODEL), row(2 * D_MODEL), row(GROUP_WIDTH)] + [_const_spec(w.shape) for w in args[3:]]
    out_specs = [row(D_MODEL)]
    out_shape = [jax.ShapeDtypeStruct((m, D_MODEL), F32)]
    nt = 0
    if sample is not None:
        *sample_args, nt = sample
        qkv, c1, c2, c3 = sample_args
        nb = c1.shape[0]
        assert m // tm == nb and qkv.shape == (nb * nt, 3 * QKV_WIDTH), "one sample batch element per token tile"
        tok = pl.BlockSpec((nt, 3 * QKV_WIDTH), lambda i: (i, 0))
        cache = lambda c: pl.BlockSpec((1,) + c.shape[1:], lambda i: (i, 0, 0))
        args += sample_args
        in_specs += [tok, cache(c1), cache(c2), cache(c3)]
        out_specs += [pl.BlockSpec((nt, GROUP_WIDTH), lambda i: (i, 0)), cache(c1), cache(c2), cache(c3)]
        out_shape += [jax.ShapeDtypeStruct((nb * nt, GROUP_WIDTH), F32)] + [jax.ShapeDtypeStruct(c.shape, F32) for c in (c1, c2, c3)]
    out = pl.pallas_call(
        functools.partial(_merge_mlp_kernel, ff_chunk=ff_chunk, nt_sample=nt),
        grid=(m // tm,),
        in_specs=in_specs,
        out_specs=tuple(out_specs),
        out_shape=tuple(out_shape),
        compiler_params=pltpu.CompilerParams(dimension_semantics=("arbitrary",),
                                             vmem_limit_bytes=VMEM_LIMIT_FUSED if sample is not None else VMEM_LIMIT),
        name="merge_mlp",
    )(*args)
    return out if sample is not None else out[0]


def _position_minor(c):
    n, w = c.shape[1], c.shape[2]
    return jnp.transpose(c[0], (0, 2, 3, 4, 1)).reshape(n, KV_ROWS, w)


def _position_major(c):
    n, _, w = c.shape
    return jnp.transpose(c.reshape(n, 2, HEADS_PER_GROUP, HEAD_DIM, w), (0, 4, 1, 2, 3))[None]


def kernel(x_prompt, x_sample, state_pool, cache_kv1, cache_kv2, cache_kv3, ln1, w_in, q_norm, k_norm,
           pool_lin, pool_scale, w_pa, w_pb, w_o, ln2, w_up, w_down):
    depth = ln1.shape[0]
    assert depth == 1
    b, s, _ = x_prompt.shape
    nb, nt, _ = x_sample.shape

    w_in16, w_pa16, w_pb16 = w_in[0].astype(BF16), w_pa[0].astype(BF16), w_pb[0].astype(BF16)
    w_o16, w_up16, w_dn16 = w_o[0].astype(BF16), w_up[0].astype(BF16), w_down[0].astype(BF16)
    ln1v, ln2v = ln1[0].reshape(1, D_MODEL), ln2[0].reshape(1, D_MODEL)
    qn, kn = q_norm[0].reshape(1, QKV_WIDTH), k_norm[0].reshape(1, QKV_WIDTH)
    scale = pool_scale[0].reshape(1, POOL_WIDTH)
    lin = pool_lin[0].astype(BF16)
    zeros = jnp.zeros((POOL_GROUP_DIM, POOL_GROUP_DIM), BF16)
    lin2 = jnp.stack([jnp.block([[lin[0], zeros], [zeros, lin[1]]]), jnp.block([[lin[2], zeros], [zeros, lin[3]]])])

    qkv, gates, kv1, kv2, kv3, pool_p = _project_prompt(x_prompt, ln1v, w_in16, qn, kn, lin2, scale, w_pa16)
    xs = x_sample.reshape(nb * nt, D_MODEL)
    state_t = jnp.transpose(state_pool[0], (1, 0, 2))
    qkv_s, gates_s, pool_s = _project_sample(xs, state_t, ln1v, w_in16, qn, kn, lin2, scale, w_pa16, nb=nb, nt=nt)

    attn = _attention_prompt(qkv)
    caches = [_position_minor(c) for c in (cache_kv1, cache_kv2, cache_kv3)]
    y_prompt, attn_s, n1, n2, n3 = _merge_mlp(
        x_prompt.reshape(b * s, D_MODEL), gates.reshape(b * s, 2 * D_MODEL), attn.reshape(b * s, GROUP_WIDTH),
        w_pb16, w_o16, ln2v, w_up16, w_dn16, sample=(qkv_s, *caches, nt))
    y_prompt = y_prompt.reshape(b, s, D_MODEL)
    y_sample = _merge_mlp(xs, gates_s, attn_s, w_pb16, w_o16, ln2v, w_up16, w_dn16).reshape(nb, nt, D_MODEL)

    return (y_prompt, y_sample,
            pool_p.reshape(1, b, POOL_STATE, POOL_WIDTH),
            _position_major(kv1), _position_major(kv2), _position_major(kv3),
            jnp.transpose(pool_s, (1, 0, 2))[None],
            _position_major(n1), _position_major(n2), _position_major(n3))
```

```python
import functools

import jax
import jax.numpy as jnp
from jax import lax
from jax.experimental import pallas as pl
from jax.experimental.pallas import tpu as pltpu

F32 = jnp.float32
BF16 = jnp.bfloat16

D_MODEL = 1024
POOL_WINDOWS = (2, 4, 8, 16)
POOL_WIDTH = 512
POOL_GROUP_DIM = 128
POOL_STATE = 15
POOL_HIST = 16
POOL_PAD = 8
ATTN_PATTERNS = ((128, 1), (512, 4), (2048, 16))
BAND = 128
HEAD_DIM = 64
HEAD_SHIFT = 6
HEADS_PER_GROUP = 4
GROUP_WIDTH = HEADS_PER_GROUP * HEAD_DIM
KV_ROWS = 2 * GROUP_WIDTH
QKV_WIDTH = 3 * GROUP_WIDTH
N_PAIRS = QKV_WIDTH // 128
Q_SLAB, K_SLAB, V_SLAB = 0, N_PAIRS, 2 * N_PAIRS
D_FF = 4096
PAST_LEN = 8192
EPS = 1e-6
COL_A, COL_Q, COL_K, COL_V, COL_GA, COL_GB = 0, 512, 1280, 2048, 2816, 3840
Q_SCALE = HEAD_DIM ** -0.5 * 1.4426950408889634
QBLOCK = 128
NEG = -1e30
LANES = 128
VMEM_LIMIT = 56 * 1024 * 1024
VMEM_LIMIT_FUSED = 62 * 1024 * 1024


def _dot(a, b):
    return jnp.dot(a, b, preferred_element_type=F32)


def _dot_nt(a, b):
    return lax.dot_general(a, b, (((1,), (1,)), ((), ())), preferred_element_type=F32)


def _rms_norm_bf16(x, w):
    ms = jnp.mean(x * x, axis=-1, keepdims=True)
    return ((x * lax.rsqrt(ms + EPS)) * w).astype(BF16)


def _sigmoid(x):
    return 1.0 / (1.0 + jnp.exp(-x))


def _head_norm(y, w):
    first = lax.broadcasted_iota(jnp.int32, (1, LANES), 1) < HEAD_DIM
    outs = []
    for g in range(3):
        halves = []
        for hp in range(2):
            cols = slice(g * GROUP_WIDTH + hp * LANES, g * GROUP_WIDTH + (hp + 1) * LANES)
            yg = y[:, cols]
            sq = yg * yg
            s0 = jnp.sum(jnp.where(first, sq, 0.0), axis=-1, keepdims=True)
            s1 = jnp.sum(jnp.where(first, 0.0, sq), axis=-1, keepdims=True)
            ms = jnp.where(first, s0, s1) * (1.0 / HEAD_DIM)
            halves.append((yg * lax.rsqrt(ms + EPS)) * w[:, cols])
        outs.append(jnp.concatenate(halves, axis=1))
    return outs


def _project_qkv(u, w_ref, qn_ref, kn_ref, store):
    q = _head_norm(_dot(u, w_ref[:, COL_Q:COL_Q + QKV_WIDTH]), qn_ref[...] * Q_SCALE)
    k = _head_norm(_dot(u, w_ref[:, COL_K:COL_K + QKV_WIDTH]), kn_ref[...])
    v = _dot(u, w_ref[:, COL_V:COL_V + QKV_WIDTH])
    for g in range(3):
        store(g, q[g], k[g], v[:, g * GROUP_WIDTH:(g + 1) * GROUP_WIDTH])


def _pool_branch_and_gates(u, diff, w_ref, lin_ref, scale_ref, wpa_ref, gates_out):
    d16 = diff.astype(BF16)
    half = POOL_WIDTH // 2
    z = jnp.concatenate([_dot(d16[:, 0:half], lin_ref[0]), _dot(d16[:, half:], lin_ref[1])], axis=1)
    a_mix = (z * scale_ref[...]).astype(BF16)
    branch_a = _dot(a_mix, wpa_ref[...])
    g_a = _dot(u, w_ref[:, COL_GA:COL_GA + D_MODEL])
    gates_out[:, 0:D_MODEL] = (_sigmoid(g_a) * branch_a).astype(BF16)
    g_b = _dot(u, w_ref[:, COL_GB:COL_GB + D_MODEL])
    gates_out[:, D_MODEL:] = _sigmoid(g_b).astype(BF16)


def _proj_prompt_kernel(x_ref, ln1_ref, w_ref, qn_ref, kn_ref, lin_ref, scale_ref, wpa_ref,
                        qkv_out, gates_out, kv1_out, kv2_out, kv3_out, pool_out,
                        lvl, *, tm):
    t = pl.program_id(1)
    last = pl.num_programs(1) - 1
    nlev = len(POOL_WINDOWS)
    top = POOL_PAD + POOL_HIST

    @pl.when(t == 0)
    def _():
        lvl[:, 0:top, :] = jnp.zeros((nlev, top, POOL_WIDTH), F32)

    u = _rms_norm_bf16(x_ref[0], ln1_ref[...])

    def store(g, q, k, v):
        for hp in range(2):
            cols = slice(hp * LANES, (hp + 1) * LANES)
            qkv_out[0, Q_SLAB + 2 * g + hp] = q[:, cols]
            qkv_out[0, K_SLAB + 2 * g + hp] = k[:, cols]
            qkv_out[0, V_SLAB + 2 * g + hp] = v[:, cols]
        if g == 2:
            kv3_out[0, 0:GROUP_WIDTH, :] = k.T
            kv3_out[0, GROUP_WIDTH:, :] = v.T

    _project_qkv(u, w_ref, qn_ref, kn_ref, store)

    lvl[0, top:top + tm, :] = _dot(u, w_ref[:, COL_A:COL_A + POOL_WIDTH])
    pos = t * tm + lax.broadcasted_iota(jnp.int32, (tm, 1), 0)
    diffs = []
    for j, w in enumerate(POOL_WINDOWS):
        assert w == 2 << j and w // 2 <= POOL_PAD
        c0 = j * POOL_GROUP_DIM
        rows = slice(POOL_PAD, top + tm)
        prev = slice(POOL_PAD - w // 2, top + tm - w // 2)
        acc = lvl[j, rows, c0:] + lvl[j, prev, c0:]
        if j + 1 < nlev:
            lvl[j + 1, rows, c0 + POOL_GROUP_DIM:] = acc[:, POOL_GROUP_DIM:]
        cnt = jnp.minimum(pos + 1, w).astype(F32)
        diffs.append(acc[POOL_HIST:, 0:POOL_GROUP_DIM] / cnt - lvl[0, top:top + tm, c0:c0 + POOL_GROUP_DIM])
    diff = jnp.concatenate(diffs, axis=1)

    _pool_branch_and_gates(u, diff, w_ref, lin_ref, scale_ref, wpa_ref, gates_out.at[0])

    @pl.when(t == last)
    def _():
        def pair_rows(slab0, g, r0):
            return jnp.concatenate([qkv_out[0, slab0 + 2 * g, r0:tm, :], qkv_out[0, slab0 + 2 * g + 1, r0:tm, :]], axis=1)
        kv2_out[0, 0:GROUP_WIDTH, :] = pair_rows(K_SLAB, 1, 0).T
        kv2_out[0, GROUP_WIDTH:, :] = pair_rows(V_SLAB, 1, 0).T
        kv1_out[0, 0:GROUP_WIDTH, :] = pair_rows(K_SLAB, 0, tm - ATTN_PATTERNS[0][0]).T
        kv1_out[0, GROUP_WIDTH:, :] = pair_rows(V_SLAB, 0, tm - ATTN_PATTERNS[0][0]).T
        pool_out[0] = lvl[0, top + tm - POOL_STATE:top + tm, :]

    lvl[0, POOL_PAD:top, :] = lvl[0, tm + POOL_PAD:tm + top, :]


def _const_spec(shape):
    nd = len(shape)
    return pl.BlockSpec(shape, lambda *_: (0,) * nd, pipeline_mode=pl.Buffered(1))


def _project_prompt(x, ln1, w_in, qn, kn, lin2, scale, w_pa, *, tm=512):
    b, s, _ = x.shape
    nt = s // tm
    assert s % tm == 0 and tm == ATTN_PATTERNS[1][0] and s == ATTN_PATTERNS[2][0]
    row = lambda width: pl.BlockSpec((1, tm, width), lambda i, j: (i, j, 0))
    tail = lambda rows, width: pl.BlockSpec((1, rows, width), lambda i, j: (i, 0, 0))
    slabs = pl.BlockSpec((1, 3 * N_PAIRS, tm, LANES), lambda i, j: (i, 0, j, 0))
    out_shape = (
        jax.ShapeDtypeStruct((b, 3 * N_PAIRS, s, LANES), F32),
        jax.ShapeDtypeStruct((b, s, 2 * D_MODEL), BF16),
        jax.ShapeDtypeStruct((b, KV_ROWS, ATTN_PATTERNS[0][0]), F32),
        jax.ShapeDtypeStruct((b, KV_ROWS, ATTN_PATTERNS[1][0]), F32),
        jax.ShapeDtypeStruct((b, KV_ROWS, s), F32),
        jax.ShapeDtypeStruct((b, POOL_STATE, POOL_WIDTH), F32),
    )
    return pl.pallas_call(
        functools.partial(_proj_prompt_kernel, tm=tm),
        grid=(b, nt),
        in_specs=[row(D_MODEL), _const_spec(ln1.shape), _const_spec(w_in.shape), _const_spec(qn.shape),
                  _const_spec(kn.shape), _const_spec(lin2.shape), _const_spec(scale.shape),
                  _const_spec(w_pa.shape)],
        out_specs=(slabs, row(2 * D_MODEL),
                   tail(KV_ROWS, ATTN_PATTERNS[0][0]), tail(KV_ROWS, ATTN_PATTERNS[1][0]),
                   pl.BlockSpec((1, KV_ROWS, tm), lambda i, j: (i, 0, j)),
                   tail(POOL_STATE, POOL_WIDTH)),
        out_shape=out_shape,
        scratch_shapes=[pltpu.VMEM((len(POOL_WINDOWS), POOL_PAD + POOL_HIST + tm, POOL_WIDTH), F32)],
        compiler_params=pltpu.CompilerParams(dimension_semantics=("arbitrary", "arbitrary"),
                                             vmem_limit_bytes=VMEM_LIMIT),
        name="project_prompt",
    )(x, ln1, w_in, qn, kn, lin2, scale, w_pa)


def _proj_sample_kernel(x_ref, st_ref, ln1_ref, w_ref, qn_ref, kn_ref, lin_ref, scale_ref, wpa_ref,
                        qkv_out, gates_out, pool_out, abuf, dbuf, *, nb, nt):
    u = _rms_norm_bf16(x_ref[...], ln1_ref[...])

    def store(g, q, k, v):
        c0 = g * GROUP_WIDTH
        qkv_out[:, c0:c0 + GROUP_WIDTH] = q
        qkv_out[:, QKV_WIDTH + c0:QKV_WIDTH + c0 + GROUP_WIDTH] = k
        qkv_out[:, 2 * QKV_WIDTH + c0:2 * QKV_WIDTH + c0 + GROUP_WIDTH] = v

    _project_qkv(u, w_ref, qn_ref, kn_ref, store)

    a = _dot(u, w_ref[:, COL_A:COL_A + POOL_WIDTH])
    for g in range(len(POOL_WINDOWS)):
        abuf[g] = a[:, g * POOL_GROUP_DIM:(g + 1) * POOL_GROUP_DIM]

    def slab(j, g):
        if j < POOL_STATE:
            return st_ref[j, :, g * POOL_GROUP_DIM:(g + 1) * POOL_GROUP_DIM]
        return abuf[g, pl.ds(j - POOL_STATE, nb, stride=nt), :]

    for t in range(nt):
        for g, w in enumerate(POOL_WINDOWS):
            a_t = slab(POOL_STATE + t, g)
            acc = a_t
            for s in range(1, w):
                acc = acc + slab(POOL_STATE + t - s, g)
            cnt = float(min(PAST_LEN + t + 1, w))
            dbuf[g, pl.ds(t, nb, stride=nt), :] = acc / cnt - a_t

    for j in range(POOL_STATE):
        pool_out[j] = jnp.concatenate([slab(j + nt, g) for g in range(len(POOL_WINDOWS))], axis=1)

    diff = jnp.concatenate([dbuf[g] for g in range(len(POOL_WINDOWS))], axis=1)
    _pool_branch_and_gates(u, diff, w_ref, lin_ref, scale_ref, wpa_ref, gates_out)


def _project_sample(x2d, state_t, ln1, w_in, qn, kn, lin2, scale, w_pa, *, nb, nt):
    m = nb * nt
    assert nt <= POOL_STATE and x2d.shape == (m, D_MODEL) and state_t.shape == (POOL_STATE, nb, POOL_WIDTH)
    args = (x2d, state_t, ln1, w_in, qn, kn, lin2, scale, w_pa)
    out_shape = (
        jax.ShapeDtypeStruct((m, 3 * QKV_WIDTH), F32),
        jax.ShapeDtypeStruct((m, 2 * D_MODEL), BF16),
        jax.ShapeDtypeStruct((POOL_STATE, nb, POOL_WIDTH), F32),
    )
    group_buf = pltpu.VMEM((len(POOL_WINDOWS), m, POOL_GROUP_DIM), F32)
    return pl.pallas_call(
        functools.partial(_proj_sample_kernel, nb=nb, nt=nt),
        grid=(1,),
        in_specs=[_const_spec(a.shape) for a in args],
        out_specs=tuple(pl.BlockSpec(o.shape, lambda i, nd=len(o.shape): (0,) * nd) for o in out_shape),
        out_shape=out_shape,
        scratch_shapes=[group_buf, group_buf],
        compiler_params=pltpu.CompilerParams(dimension_semantics=("arbitrary",), vmem_limit_bytes=VMEM_LIMIT),
        name="project_sample",
    )(*args)


def _rows(start, n, stride):
    return pl.ds(start, n) if stride == 1 else pl.ds(start, n, stride=stride)


def _band_bias(nk, offset):
    qi = lax.broadcasted_iota(jnp.int32, (2 * QBLOCK, nk), 0) & (QBLOCK - 1)
    kj = lax.broadcasted_iota(jnp.int32, (2 * QBLOCK, nk), 1)
    dist = qi + offset - kj
    return jnp.where((dist >= 0) & (dist <= BAND), 0.0, NEG).astype(F32)


def _attn_prompt_kernel(qkv_ref, o_ref, o1_buf, l1_buf, o2_buf, l2_buf, s_buf, bias_buf, *, seq):
    lane = lax.broadcasted_iota(jnp.int32, (QBLOCK, LANES), 1)
    first = lane < HEAD_DIM
    bias_buf[0] = _band_bias(2 * QBLOCK, 0)
    bias_buf[1] = _band_bias(2 * QBLOCK, QBLOCK)

    def scores(slot, hp, qstart, kstart, nk, dil, pair0, bias_sel):
        qrows, krows = _rows(qstart, QBLOCK, dil), _rows(kstart, nk, dil)
        q16 = qkv_ref[0, Q_SLAB + pair0 + hp, qrows, :].astype(BF16)
        zero = jnp.zeros_like(q16)
        q2 = jnp.concatenate([jnp.where(first, q16, zero), jnp.where(first, zero, q16)], axis=0)
        s = _dot_nt(q2, qkv_ref[0, K_SLAB + pair0 + hp, krows, :].astype(BF16))
        s_buf[slot, hp, :, 0:nk] = s + bias_buf[bias_sel, :, 0:nk]

    def values(slot, hp, qstart, kstart, nk, dil, pair0, emit):
        qrows, krows = _rows(qstart, QBLOCK, dil), _rows(kstart, nk, dil)
        s = s_buf[slot, hp, :, 0:nk]
        m = jnp.max(s, axis=-1, keepdims=True)
        p = jnp.exp2(s - m).astype(BF16)
        v_ext = jnp.concatenate([qkv_ref[0, V_SLAB + pair0 + hp, krows, :].astype(BF16), jnp.ones((nk, LANES), BF16)], axis=1)
        o = _dot(p, v_ext)
        num = jnp.where(first, o[:QBLOCK, :LANES], o[QBLOCK:, :LANES])
        den = jnp.where(first, o[:QBLOCK, LANES:], o[QBLOCK:, LANES:])
        lse = jnp.where(first, m[:QBLOCK], m[QBLOCK:]) + jnp.log2(den)
        emit(qrows, hp, num / den, lse)

    def run(tasks):
        assert len(tasks) % 4 == 0

        def stage(score_slot, i_score, value_slot, i_value):
            for d in range(2):
                for hp in range(2):
                    if i_score + d < len(tasks):
                        qs, ks, sel, nk, dil, pair0, _ = tasks[i_score + d]
                        scores(score_slot + d, hp, qs, ks, nk, dil, pair0, sel)
                    qs, ks, _, nk, dil, pair0, emit = tasks[i_value + d]
                    values(value_slot + d, hp, qs, ks, nk, dil, pair0, emit)

        for d in range(2):
            for hp in range(2):
                qs, ks, sel, nk, dil, pair0, _ = tasks[d]
                scores(d, hp, qs, ks, nk, dil, pair0, sel)
        for i0 in range(0, len(tasks), 4):
            stage(2, i0 + 2, 0, i0)
            stage(0, i0 + 4, 2, i0 + 2)

    def store_to(o_buf, l_buf):
        def emit(qrows, hp, out, lse):
            o_buf[hp, qrows, :] = out
            l_buf[hp, qrows, :] = lse
        return emit

    def subsequence_tasks(dil, nk, pair0, emit):
        span = dil * QBLOCK
        return [(r + nb * span, r + max(nb - 1, 0) * span, min(nb, 1), nk, dil, pair0, emit)
                for nb in range(seq // span) for r in range(dil)]

    def combine(qrows, hp, out0, lse0):
        lse1, lse2 = l1_buf[hp, qrows, :], l2_buf[hp, qrows, :]
        m = jnp.maximum(jnp.maximum(lse0, lse1), lse2)
        w0, w1, w2 = jnp.exp2(lse0 - m), jnp.exp2(lse1 - m), jnp.exp2(lse2 - m)
        num = w0 * out0 + w1 * o1_buf[hp, qrows, :] + w2 * o2_buf[hp, qrows, :]
        o_ref[0, qrows, hp * LANES:(hp + 1) * LANES] = (num / (w0 + w1 + w2)).astype(o_ref.dtype)

    assert seq == 16 * QBLOCK
    wide = subsequence_tasks(16, QBLOCK, 4, store_to(o2_buf, l2_buf))
    mid = subsequence_tasks(4, 2 * QBLOCK, 2, store_to(o1_buf, l1_buf))
    near = subsequence_tasks(1, 2 * QBLOCK, 0, combine)
    assert len(mid) == len(near) and len(mid) % 4 == 0
    mixed = mid[0:4]
    for i in range(0, len(near), 2):
        mixed += near[i:i + 2] + mid[i + 4:i + 6]
    run(wide + mixed)


def _attention_prompt(qkv):
    b, _, s, _ = qkv.shape
    spec = pl.BlockSpec((1, 3 * N_PAIRS, s, LANES), lambda i: (i, 0, 0, 0))
    group_buf = pltpu.VMEM((2, s, LANES), F32)
    return pl.pallas_call(
        functools.partial(_attn_prompt_kernel, seq=s),
        grid=(b,),
        in_specs=[spec],
        out_specs=pl.BlockSpec((1, s, GROUP_WIDTH), lambda i: (i, 0, 0)),
        out_shape=jax.ShapeDtypeStruct((b, s, GROUP_WIDTH), BF16),
        scratch_shapes=[group_buf, group_buf, group_buf, group_buf,
                        pltpu.VMEM((4, 2, 2 * QBLOCK, 2 * QBLOCK), F32),
                        pltpu.VMEM((2, 2 * QBLOCK, 2 * QBLOCK), F32)],
        compiler_params=pltpu.CompilerParams(dimension_semantics=("arbitrary",), vmem_limit_bytes=VMEM_LIMIT),
        name="attention_prompt",
    )(qkv)


def _attn_sample_parts(qkv_ref, c1_ref, c2_ref, c3_ref, o_ref, n1_ref, n2_ref, n3_ref, *, nt):
    nrow = HEADS_PER_GROUP * nt
    nt_shift = nt.bit_length() - 1
    assert nt == 1 << nt_shift
    row_head = lax.broadcasted_iota(jnp.int32, (nrow, GROUP_WIDTH), 0) >> nt_shift
    col_head = lax.broadcasted_iota(jnp.int32, (nrow, GROUP_WIDTH), 1) >> HEAD_SHIFT
    own = row_head == col_head
    col_head_t = lax.broadcasted_iota(jnp.int32, (nt, GROUP_WIDTH), 1) >> HEAD_SHIFT
    tail_lane = lax.broadcasted_iota(jnp.int32, (KV_ROWS, LANES), 1) >= LANES - nt

    def own_blocks(x):
        acc = jnp.zeros((nt, GROUP_WIDTH), F32)
        for h in range(HEADS_PER_GROUP):
            acc = acc + jnp.where(col_head_t == h, x[h * nt:(h + 1) * nt], 0.0)
        return acc

    def group(g):
        c_ref, n_ref = ((c1_ref, n1_ref), (c2_ref, n2_ref), (c3_ref, n3_ref))[g]
        win, dil = ATTN_PATTERNS[g]
        assert c_ref.shape == (1, KV_ROWS, win) and dil & (dil - 1) == 0
        cs = g * GROUP_WIDTH
        k_new = qkv_ref[:, QKV_WIDTH + cs:QKV_WIDTH + cs + GROUP_WIDTH]
        v_new = qkv_ref[:, 2 * QKV_WIDTH + cs:2 * QKV_WIDTH + cs + GROUP_WIDTH]

        new_t = jnp.concatenate([jnp.zeros((LANES - nt, KV_ROWS), F32), jnp.concatenate([k_new, v_new], axis=1)],
                                axis=0).T
        ntiles = win // LANES
        rot = pltpu.roll(c_ref[0, :, 0:LANES], LANES - nt, axis=1)
        for j in range(ntiles):
            nxt = pltpu.roll(c_ref[0, :, (j + 1) * LANES:(j + 2) * LANES], LANES - nt, axis=1) if j + 1 < ntiles else new_t
            n_ref[0, :, j * LANES:(j + 1) * LANES] = jnp.where(tail_lane, nxt, rot)
            rot = nxt

        q = qkv_ref[:, cs:cs + GROUP_WIDTH]
        qbd = jnp.where(own, jnp.concatenate([q] * HEADS_PER_GROUP, axis=0), 0.0).astype(BF16)
        k_old = c_ref[0, 0:GROUP_WIDTH, :].astype(BF16)
        v_old = c_ref[0, GROUP_WIDTH:, :].astype(BF16)
        t_old = lax.broadcasted_iota(jnp.int32, (nrow, win), 0) & (nt - 1)
        i_old = lax.broadcasted_iota(jnp.int32, (nrow, win), 1)
        ok_old = (i_old >= t_old) & (((i_old - t_old) & (dil - 1)) == 0)
        t_new = lax.broadcasted_iota(jnp.int32, (nrow, nt), 0) & (nt - 1)
        i_new = lax.broadcasted_iota(jnp.int32, (nrow, nt), 1)
        ok_new = (i_new <= t_new) & (((t_new - i_new) & (dil - 1)) == 0)
        s_old = _dot(qbd, k_old) + jnp.where(ok_old, 0.0, NEG)
        s_new = _dot_nt(qbd, k_new.astype(BF16)) + jnp.where(ok_new, 0.0, NEG)
        m = jnp.maximum(jnp.max(s_old, axis=-1, keepdims=True), jnp.max(s_new, axis=-1, keepdims=True))
        p_old = jnp.exp2(s_old - m)
        p_new = jnp.exp2(s_new - m)
        den = jnp.sum(p_old, axis=-1, keepdims=True) + jnp.sum(p_new, axis=-1, keepdims=True)
        o = (_dot_nt(p_old.astype(BF16), v_old) + _dot(p_new.astype(BF16), v_new.astype(BF16))) / den
        return own_blocks(o), own_blocks(jnp.broadcast_to(m + jnp.log2(den), (nrow, GROUP_WIDTH)))

    def combine(parts):
        (o0, l0), (o1, l1), (o2, l2) = parts
        m = jnp.maximum(jnp.maximum(l0, l1), l2)
        w0, w1, w2 = jnp.exp2(l0 - m), jnp.exp2(l1 - m), jnp.exp2(l2 - m)
        o_ref[...] = (w0 * o0 + w1 * o1 + w2 * o2) / (w0 + w1 + w2)

    return group, combine


N_MERGE_IN = 8
N_SAMPLE_IN = 4


def _merge_mlp_kernel(*refs, ff_chunk, nt_sample):
    x_ref, gates_ref, at_ref, wpb_ref, wo_ref, ln2_ref, wup_ref, wdn_ref = refs[:N_MERGE_IN]
    y_ref = refs[N_MERGE_IN + (N_SAMPLE_IN if nt_sample else 0)]
    pending = []
    if nt_sample:
        group, combine = _attn_sample_parts(*refs[N_MERGE_IN:N_MERGE_IN + N_SAMPLE_IN],
                                            *refs[N_MERGE_IN + N_SAMPLE_IN + 1:], nt=nt_sample)
        parts = [None] * len(ATTN_PATTERNS)

        def run_group(g):
            parts[g] = group(g)

        def run_combine():
            combine(parts)
        pending = [functools.partial(run_group, g) for g in (2, 1, 0)] + [run_combine]

    def issue_sample_piece():
        if pending:
            pending.pop(0)()

    issue_sample_piece()
    branch_b = _dot(at_ref[...].astype(BF16), wpb_ref[...])
    mixed = (gates_ref[:, 0:D_MODEL].astype(F32) + gates_ref[:, D_MODEL:].astype(F32) * branch_b).astype(BF16)
    h = x_ref[...] + _dot(mixed, wo_ref[...])
    u = _rms_norm_bf16(h, ln2_ref[...])
    acc = h
    for c in range(0, D_FF, ff_chunk):
        z = jnp.maximum(_dot(u, wup_ref[:, c:c + ff_chunk]), 0.0)
        acc = acc + _dot((z * z).astype(BF16), wdn_ref[c:c + ff_chunk, :])
        issue_sample_piece()
    y_ref[...] = acc
    while pending:
        issue_sample_piece()


def _merge_mlp(x2d, gates, attn, w_pb, w_o, ln2, w_up, w_down, sample=None, *, tm=512, ff_chunk=2048):
    m = x2d.shape[0]
    tm = min(tm, m)
    assert m % tm == 0
    row = lambda width: pl.BlockSpec((tm, width), lambda i: (i, 0))
    args = [x2d, gates, attn, w_pb, w_o, ln2, w_up, w_down]
    in_specs = [row(D_MODEL), row(2 * D_MODEL), row(GROUP_WIDTH)] + [_const_spec(w.shape) for w in args[3:]]
    out_specs = [row(D_MODEL)]
    out_shape = [jax.ShapeDtypeStruct((m, D_MODEL), F32)]
    nt = 0
    if sample is not None:
        *sample_args, nt = sample
        qkv, c1, c2, c3 = sample_args
        nb = c1.shape[0]
        assert m // tm == nb and qkv.shape == (nb * nt, 3 * QKV_WIDTH), "one sample batch element per token tile"
        tok = pl.BlockSpec((nt, 3 * QKV_WIDTH), lambda i: (i, 0))
        cache = lambda c: pl.BlockSpec((1,) + c.shape[1:], lambda i: (i, 0, 0))
        args += sample_args
        in_specs += [tok, cache(c1), cache(c2), cache(c3)]
        out_specs += [pl.BlockSpec((nt, GROUP_WIDTH), lambda i: (i, 0)), cache(c1), cache(c2), cache(c3)]
        out_shape += [jax.ShapeDtypeStruct((nb * nt, GROUP_WIDTH), F32)] + [jax.ShapeDtypeStruct(c.shape, F32) for c in (c1, c2, c3)]
    out = pl.pallas_call(
        functools.partial(_merge_mlp_kernel, ff_chunk=ff_chunk, nt_sample=nt),
        grid=(m // tm,),
        in_specs=in_specs,
        out_specs=tuple(out_specs),
        out_shape=tuple(out_shape),
        compiler_params=pltpu.CompilerParams(dimension_semantics=("arbitrary",),
                                             vmem_limit_bytes=VMEM_LIMIT_FUSED if sample is not None else VMEM_LIMIT),
        name="merge_mlp",
    )(*args)
    return out if sample is not None else out[0]


def _position_minor(c):
    n, w = c.shape[1], c.shape[2]
    return jnp.transpose(c[0], (0, 2, 3, 4, 1)).reshape(n, KV_ROWS, w)


def _position_major(c):
    n, _, w = c.shape
    return jnp.transpose(c.reshape(n, 2, HEADS_PER_GROUP, HEAD_DIM, w), (0, 4, 1, 2, 3))[None]


def kernel(x_prompt, x_sample, state_pool, cache_kv1, cache_kv2, cache_kv3, ln1, w_in, q_norm, k_norm,
           pool_lin, pool_scale, w_pa, w_pb, w_o, ln2, w_up, w_down):
    depth = ln1.shape[0]
    assert depth == 1
    b, s, _ = x_prompt.shape
    nb, nt, _ = x_sample.shape

    w_in16, w_pa16, w_pb16 = w_in[0].astype(BF16), w_pa[0].astype(BF16), w_pb[0].astype(BF16)
    w_o16, w_up16, w_dn16 = w_o[0].astype(BF16), w_up[0].astype(BF16), w_down[0].astype(BF16)
    ln1v, ln2v = ln1[0].reshape(1, D_MODEL), ln2[0].reshape(1, D_MODEL)
    qn, kn = q_norm[0].reshape(1, QKV_WIDTH), k_norm[0].reshape(1, QKV_WIDTH)
    scale = pool_scale[0].reshape(1, POOL_WIDTH)
    lin = pool_lin[0].astype(BF16)
    zeros = jnp.zeros((POOL_GROUP_DIM, POOL_GROUP_DIM), BF16)
    lin2 = jnp.stack([jnp.block([[lin[0], zeros], [zeros, lin[1]]]), jnp.block([[lin[2], zeros], [zeros, lin[3]]])])

    qkv, gates, kv1, kv2, kv3, pool_p = _project_prompt(x_prompt, ln1v, w_in16, qn, kn, lin2, scale, w_pa16)
    xs = x_sample.reshape(nb * nt, D_MODEL)
    state_t = jnp.transpose(state_pool[0], (1, 0, 2))
    qkv_s, gates_s, pool_s = _project_sample(xs, state_t, ln1v, w_in16, qn, kn, lin2, scale, w_pa16, nb=nb, nt=nt)

    attn = _attention_prompt(qkv)
    caches = [_position_minor(c) for c in (cache_kv1, cache_kv2, cache_kv3)]
    y_prompt, attn_s, n1, n2, n3 = _merge_mlp(
        x_prompt.reshape(b * s, D_MODEL), gates.reshape(b * s, 2 * D_MODEL), attn.reshape(b * s, GROUP_WIDTH),
        w_pb16, w_o16, ln2v, w_up16, w_dn16, sample=(qkv_s, *caches, nt))
    y_prompt = y_prompt.reshape(b, s, D_MODEL)
    y_sample = _merge_mlp(xs, gates_s, attn_s, w_pb16, w_o16, ln2v, w_up16, w_dn16).reshape(nb, nt, D_MODEL)

    return (y_prompt, y_sample,
            pool_p.reshape(1, b, POOL_STATE, POOL_WIDTH),
            _position_major(kv1), _position_major(kv2), _position_major(kv3),
            jnp.transpose(pool_s, (1, 0, 2))[None],
            _position_major(n1), _position_major(n2), _position_major(n3))
```

```python
import functools

import jax
import jax.numpy as jnp
from jax import lax
from jax.experimental import pallas as pl
from jax.experimental.pallas import tpu as pltpu

F32 = jnp.float32
BF16 = jnp.bfloat16

D_MODEL = 1024
POOL_WINDOWS = (2, 4, 8, 16)
POOL_WIDTH = 512
POOL_GROUP_DIM = 128
POOL_STATE = 15
POOL_HIST = 16
POOL_PAD = 8
ATTN_PATTERNS = ((128, 1), (512, 4), (2048, 16))
BAND = 128
HEAD_DIM = 64
HEAD_SHIFT = 6
HEADS_PER_GROUP = 4
GROUP_WIDTH = HEADS_PER_GROUP * HEAD_DIM
KV_ROWS = 2 * GROUP_WIDTH
QKV_WIDTH = 3 * GROUP_WIDTH
N_PAIRS = QKV_WIDTH // 128
Q_SLAB, K_SLAB, V_SLAB = 0, N_PAIRS, 2 * N_PAIRS
D_FF = 4096
PAST_LEN = 8192
EPS = 1e-6
COL_A, COL_Q, COL_K, COL_V, COL_GA, COL_GB = 0, 512, 1280, 2048, 2816, 3840
Q_SCALE = HEAD_DIM ** -0.5 * 1.4426950408889634
QBLOCK = 128
NEG = -1e30
LANES = 128
VMEM_LIMIT = 56 * 1024 * 1024
VMEM_LIMIT_FUSED = 62 * 1024 * 1024


def _dot(a, b):
    return jnp.dot(a, b, preferred_element_type=F32)


def _dot_nt(a, b):
    return lax.dot_general(a, b, (((1,), (1,)), ((), ())), preferred_element_type=F32)


def _rms_norm_bf16(x, w):
    ms = jnp.mean(x * x, axis=-1, keepdims=True)
    return ((x * lax.rsqrt(ms + EPS)) * w).astype(BF16)


def _sigmoid(x):
    return 1.0 / (1.0 + jnp.exp(-x))


def _head_norm(y, w):
    first = lax.broadcasted_iota(jnp.int32, (1, LANES), 1) < HEAD_DIM
    outs = []
    for g in range(3):
        halves = []
        for hp in range(2):
            cols = slice(g * GROUP_WIDTH + hp * LANES, g * GROUP_WIDTH + (hp + 1) * LANES)
            yg = y[:, cols]
            sq = yg * yg
            s0 = jnp.sum(jnp.where(first, sq, 0.0), axis=-1, keepdims=True)
            s1 = jnp.sum(jnp.where(first, 0.0, sq), axis=-1, keepdims=True)
            ms = jnp.where(first, s0, s1) * (1.0 / HEAD_DIM)
            halves.append((yg * lax.rsqrt(ms + EPS)) * w[:, cols])
        outs.append(jnp.concatenate(halves, axis=1))
    return outs


def _project_qkv(u, w_ref, qn_ref, kn_ref, store):
    q = _head_norm(_dot(u, w_ref[:, COL_Q:COL_Q + QKV_WIDTH]), qn_ref[...] * Q_SCALE)
    k = _head_norm(_dot(u, w_ref[:, COL_K:COL_K + QKV_WIDTH]), kn_ref[...])
    v = _dot(u, w_ref[:, COL_V:COL_V + QKV_WIDTH])
    for g in range(3):
        store(g, q[g], k[g], v[:, g * GROUP_WIDTH:(g + 1) * GROUP_WIDTH])


def _gate_logits(u, w_ref):
    return _dot(u, w_ref[:, COL_GA:COL_GA + D_MODEL]), _dot(u, w_ref[:, COL_GB:COL_GB + D_MODEL])


def _pool_branch_and_gates(diff, g_a, g_b, lin_ref, scale_ref, wpa_ref, gates_out):
    d16 = diff.astype(BF16)
    half = POOL_WIDTH // 2
    z = jnp.concatenate([_dot(d16[:, 0:half], lin_ref[0]), _dot(d16[:, half:], lin_ref[1])], axis=1)
    a_mix = (z * scale_ref[...]).astype(BF16)
    branch_a = _dot(a_mix, wpa_ref[...])
    gates_out[:, 0:D_MODEL] = (_sigmoid(g_a) * branch_a).astype(BF16)
    gates_out[:, D_MODEL:] = _sigmoid(g_b).astype(BF16)


def _proj_prompt_kernel(x_ref, ln1_ref, w_ref, qn_ref, kn_ref, lin_ref, scale_ref, wpa_ref,
                        qkv_out, gates_out, kv1_out, kv2_out, kv3_out, pool_out,
                        lvl, *, tm):
    t = pl.program_id(1)
    last = pl.num_programs(1) - 1
    nlev = len(POOL_WINDOWS)
    top = POOL_PAD + POOL_HIST

    @pl.when(t == 0)
    def _():
        lvl[:, 0:top, :] = jnp.zeros((nlev, top, POOL_WIDTH), F32)

    u = _rms_norm_bf16(x_ref[0], ln1_ref[...])

    def store(g, q, k, v):
        for hp in range(2):
            cols = slice(hp * LANES, (hp + 1) * LANES)
            qkv_out[0, Q_SLAB + 2 * g + hp] = q[:, cols]
            qkv_out[0, K_SLAB + 2 * g + hp] = k[:, cols]
            qkv_out[0, V_SLAB + 2 * g + hp] = v[:, cols]
        if g == 2:
            kv3_out[0, 0:GROUP_WIDTH, :] = k.T
            kv3_out[0, GROUP_WIDTH:, :] = v.T

    _project_qkv(u, w_ref, qn_ref, kn_ref, store)

    g_a, g_b = _gate_logits(u, w_ref)

    lvl[0, top:top + tm, :] = _dot(u, w_ref[:, COL_A:COL_A + POOL_WIDTH])
    pos = t * tm + lax.broadcasted_iota(jnp.int32, (tm, 1), 0)
    diffs = []
    for j, w in enumerate(POOL_WINDOWS):
        assert w == 2 << j and w // 2 <= POOL_PAD
        c0 = j * POOL_GROUP_DIM
        rows = slice(POOL_PAD, top + tm)
        prev = slice(POOL_PAD - w // 2, top + tm - w // 2)
        acc = lvl[j, rows, c0:] + lvl[j, prev, c0:]
        if j + 1 < nlev:
            lvl[j + 1, rows, c0 + POOL_GROUP_DIM:] = acc[:, POOL_GROUP_DIM:]
        cnt = jnp.minimum(pos + 1, w).astype(F32)
        diffs.append(acc[POOL_HIST:, 0:POOL_GROUP_DIM] / cnt - lvl[0, top:top + tm, c0:c0 + POOL_GROUP_DIM])
    diff = jnp.concatenate(diffs, axis=1)

    _pool_branch_and_gates(diff, g_a, g_b, lin_ref, scale_ref, wpa_ref, gates_out.at[0])

    @pl.when(t == last)
    def _():
        def pair_rows(slab0, g, r0):
            return jnp.concatenate([qkv_out[0, slab0 + 2 * g, r0:tm, :], qkv_out[0, slab0 + 2 * g + 1, r0:tm, :]], axis=1)
        kv2_out[0, 0:GROUP_WIDTH, :] = pair_rows(K_SLAB, 1, 0).T
        kv2_out[0, GROUP_WIDTH:, :] = pair_rows(V_SLAB, 1, 0).T
        kv1_out[0, 0:GROUP_WIDTH, :] = pair_rows(K_SLAB, 0, tm - ATTN_PATTERNS[0][0]).T
        kv1_out[0, GROUP_WIDTH:, :] = pair_rows(V_SLAB, 0, tm - ATTN_PATTERNS[0][0]).T
        pool_out[0] = lvl[0, top + tm - POOL_STATE:top + tm, :]

    lvl[0, POOL_PAD:top, :] = lvl[0, tm + POOL_PAD:tm + top, :]


def _const_spec(shape):
    nd = len(shape)
    return pl.BlockSpec(shape, lambda *_: (0,) * nd, pipeline_mode=pl.Buffered(1))


def _project_prompt(x, ln1, w_in, qn, kn, lin2, scale, w_pa, *, tm=512):
    b, s, _ = x.shape
    nt = s // tm
    assert s % tm == 0 and tm == ATTN_PATTERNS[1][0] and s == ATTN_PATTERNS[2][0]
    row = lambda width: pl.BlockSpec((1, tm, width), lambda i, j: (i, j, 0))
    tail = lambda rows, width: pl.BlockSpec((1, rows, width), lambda i, j: (i, 0, 0))
    slabs = pl.BlockSpec((1, 3 * N_PAIRS, tm, LANES), lambda i, j: (i, 0, j, 0))
    out_shape = (
        jax.ShapeDtypeStruct((b, 3 * N_PAIRS, s, LANES), F32),
        jax.ShapeDtypeStruct((b, s, 2 * D_MODEL), BF16),
        jax.ShapeDtypeStruct((b, KV_ROWS, ATTN_PATTERNS[0][0]), F32),
        jax.ShapeDtypeStruct((b, KV_ROWS, ATTN_PATTERNS[1][0]), F32),
        jax.ShapeDtypeStruct((b, KV_ROWS, s), F32),
        jax.ShapeDtypeStruct((b, POOL_STATE, POOL_WIDTH), F32),
    )
    return pl.pallas_call(
        functools.partial(_proj_prompt_kernel, tm=tm),
        grid=(b, nt),
        in_specs=[row(D_MODEL), _const_spec(ln1.shape), _const_spec(w_in.shape), _const_spec(qn.shape),
                  _const_spec(kn.shape), _const_spec(lin2.shape), _const_spec(scale.shape),
                  _const_spec(w_pa.shape)],
        out_specs=(slabs, row(2 * D_MODEL),
                   tail(KV_ROWS, ATTN_PATTERNS[0][0]), tail(KV_ROWS, ATTN_PATTERNS[1][0]),
                   pl.BlockSpec((1, KV_ROWS, tm), lambda i, j: (i, 0, j)),
                   tail(POOL_STATE, POOL_WIDTH)),
        out_shape=out_shape,
        scratch_shapes=[pltpu.VMEM((len(POOL_WINDOWS), POOL_PAD + POOL_HIST + tm, POOL_WIDTH), F32)],
        compiler_params=pltpu.CompilerParams(dimension_semantics=("arbitrary", "arbitrary"),
                                             vmem_limit_bytes=VMEM_LIMIT),
        name="project_prompt",
    )(x, ln1, w_in, qn, kn, lin2, scale, w_pa)


def _proj_sample_kernel(x_ref, st_ref, ln1_ref, w_ref, qn_ref, kn_ref, lin_ref, scale_ref, wpa_ref,
                        qkv_out, gates_out, pool_out, abuf, dbuf, *, nb, nt):
    u = _rms_norm_bf16(x_ref[...], ln1_ref[...])

    def store(g, q, k, v):
        c0 = g * GROUP_WIDTH
        qkv_out[:, c0:c0 + GROUP_WIDTH] = q
        qkv_out[:, QKV_WIDTH + c0:QKV_WIDTH + c0 + GROUP_WIDTH] = k
        qkv_out[:, 2 * QKV_WIDTH + c0:2 * QKV_WIDTH + c0 + GROUP_WIDTH] = v

    _project_qkv(u, w_ref, qn_ref, kn_ref, store)

    a = _dot(u, w_ref[:, COL_A:COL_A + POOL_WIDTH])
    for g in range(len(POOL_WINDOWS)):
        abuf[g] = a[:, g * POOL_GROUP_DIM:(g + 1) * POOL_GROUP_DIM]

    def slab(j, g):
        if j < POOL_STATE:
            return st_ref[j, :, g * POOL_GROUP_DIM:(g + 1) * POOL_GROUP_DIM]
        return abuf[g, pl.ds(j - POOL_STATE, nb, stride=nt), :]

    for t in range(nt):
        for g, w in enumerate(POOL_WINDOWS):
            a_t = slab(POOL_STATE + t, g)
            acc = a_t
            for s in range(1, w):
                acc = acc + slab(POOL_STATE + t - s, g)
            cnt = float(min(PAST_LEN + t + 1, w))
            dbuf[g, pl.ds(t, nb, stride=nt), :] = acc / cnt - a_t

    for j in range(POOL_STATE):
        pool_out[j] = jnp.concatenate([slab(j + nt, g) for g in range(len(POOL_WINDOWS))], axis=1)

    diff = jnp.concatenate([dbuf[g] for g in range(len(POOL_WINDOWS))], axis=1)
    _pool_branch_and_gates(diff, *_gate_logits(u, w_ref), lin_ref, scale_ref, wpa_ref, gates_out)


def _project_sample(x2d, state_t, ln1, w_in, qn, kn, lin2, scale, w_pa, *, nb, nt):
    m = nb * nt
    assert nt <= POOL_STATE and x2d.shape == (m, D_MODEL) and state_t.shape == (POOL_STATE, nb, POOL_WIDTH)
    args = (x2d, state_t, ln1, w_in, qn, kn, lin2, scale, w_pa)
    out_shape = (
        jax.ShapeDtypeStruct((m, 3 * QKV_WIDTH), F32),
        jax.ShapeDtypeStruct((m, 2 * D_MODEL), BF16),
        jax.ShapeDtypeStruct((POOL_STATE, nb, POOL_WIDTH), F32),
    )
    group_buf = pltpu.VMEM((len(POOL_WINDOWS), m, POOL_GROUP_DIM), F32)
    return pl.pallas_call(
        functools.partial(_proj_sample_kernel, nb=nb, nt=nt),
        grid=(1,),
        in_specs=[_const_spec(a.shape) for a in args],
        out_specs=tuple(pl.BlockSpec(o.shape, lambda i, nd=len(o.shape): (0,) * nd) for o in out_shape),
        out_shape=out_shape,
        scratch_shapes=[group_buf, group_buf],
        compiler_params=pltpu.CompilerParams(dimension_semantics=("arbitrary",), vmem_limit_bytes=VMEM_LIMIT),
        name="project_sample",
    )(*args)


def _rows(start, n, stride):
    return pl.ds(start, n) if stride == 1 else pl.ds(start, n, stride=stride)


def _band_bias(nk, offset):
    qi = lax.broadcasted_iota(jnp.int32, (2 * QBLOCK, nk), 0) & (QBLOCK - 1)
    kj = lax.broadcasted_iota(jnp.int32, (2 * QBLOCK, nk), 1)
    dist = qi + offset - kj
    return jnp.where((dist >= 0) & (dist <= BAND), 0.0, NEG).astype(F32)


def _attn_prompt_kernel(qkv_ref, o_ref, o1_buf, l1_buf, o2_buf, l2_buf, s_buf, bias_buf, *, seq):
    lane = lax.broadcasted_iota(jnp.int32, (QBLOCK, LANES), 1)
    first = lane < HEAD_DIM
    bias_buf[0] = _band_bias(2 * QBLOCK, 0)
    bias_buf[1] = _band_bias(2 * QBLOCK, QBLOCK)

    def scores(slot, hp, qstart, kstart, nk, dil, pair0, bias_sel):
        qrows, krows = _rows(qstart, QBLOCK, dil), _rows(kstart, nk, dil)
        q16 = qkv_ref[0, Q_SLAB + pair0 + hp, qrows, :].astype(BF16)
        zero = jnp.zeros_like(q16)
        q2 = jnp.concatenate([jnp.where(first, q16, zero), jnp.where(first, zero, q16)], axis=0)
        s = _dot_nt(q2, qkv_ref[0, K_SLAB + pair0 + hp, krows, :].astype(BF16))
        s_buf[slot, hp, :, 0:nk] = s + bias_buf[bias_sel, :, 0:nk]

    def values(slot, hp, qstart, kstart, nk, dil, pair0, emit):
        qrows, krows = _rows(qstart, QBLOCK, dil), _rows(kstart, nk, dil)
        s = s_buf[slot, hp, :, 0:nk]
        m = jnp.max(s, axis=-1, keepdims=True)
        p = jnp.exp2(s - m).astype(BF16)
        v_ext = jnp.concatenate([qkv_ref[0, V_SLAB + pair0 + hp, krows, :].astype(BF16), jnp.ones((nk, LANES), BF16)], axis=1)
        o = _dot(p, v_ext)
        num = jnp.where(first, o[:QBLOCK, :LANES], o[QBLOCK:, :LANES])
        den = jnp.where(first, o[:QBLOCK, LANES:], o[QBLOCK:, LANES:])
        lse = jnp.where(first, m[:QBLOCK], m[QBLOCK:]) + jnp.log2(den)
        emit(qrows, hp, num / den, lse)

    def run(tasks):
        assert len(tasks) % 4 == 0

        def stage(score_slot, i_score, value_slot, i_value):
            for d in range(2):
                for hp in range(2):
                    if i_score + d < len(tasks):
                        qs, ks, sel, nk, dil, pair0, _ = tasks[i_score + d]
                        scores(score_slot + d, hp, qs, ks, nk, dil, pair0, sel)
                    qs, ks, _, nk, dil, pair0, emit = tasks[i_value + d]
                    values(value_slot + d, hp, qs, ks, nk, dil, pair0, emit)

        for d in range(2):
            for hp in range(2):
                qs, ks, sel, nk, dil, pair0, _ = tasks[d]
                scores(d, hp, qs, ks, nk, dil, pair0, sel)
        for i0 in range(0, len(tasks), 4):
            stage(2, i0 + 2, 0, i0)
            stage(0, i0 + 4, 2, i0 + 2)

    def store_to(o_buf, l_buf):
        def emit(qrows, hp, out, lse):
            o_buf[hp, qrows, :] = out
            l_buf[hp, qrows, :] = lse
        return emit

    def subsequence_tasks(dil, nk, pair0, emit):
        span = dil * QBLOCK
        return [(r + nb * span, r + max(nb - 1, 0) * span, min(nb, 1), nk, dil, pair0, emit)
                for nb in range(seq // span) for r in range(dil)]

    def combine(qrows, hp, out0, lse0):
        lse1, lse2 = l1_buf[hp, qrows, :], l2_buf[hp, qrows, :]
        m = jnp.maximum(jnp.maximum(lse0, lse1), lse2)
        w0, w1, w2 = jnp.exp2(lse0 - m), jnp.exp2(lse1 - m), jnp.exp2(lse2 - m)
        num = w0 * out0 + w1 * o1_buf[hp, qrows, :] + w2 * o2_buf[hp, qrows, :]
        o_ref[0, qrows, hp * LANES:(hp + 1) * LANES] = (num / (w0 + w1 + w2)).astype(o_ref.dtype)

    assert seq == 16 * QBLOCK
    wide = subsequence_tasks(16, QBLOCK, 4, store_to(o2_buf, l2_buf))
    mid = subsequence_tasks(4, 2 * QBLOCK, 2, store_to(o1_buf, l1_buf))
    near = subsequence_tasks(1, 2 * QBLOCK, 0, combine)
    assert len(mid) == len(near) and len(mid) % 4 == 0
    mixed = mid[0:4]
    for i in range(0, len(near), 2):
        mixed += near[i:i + 2] + mid[i + 4:i + 6]
    run(wide + mixed)


def _attention_prompt(qkv):
    b, _, s, _ = qkv.shape
    spec = pl.BlockSpec((1, 3 * N_PAIRS, s, LANES), lambda i: (i, 0, 0, 0))
    group_buf = pltpu.VMEM((2, s, LANES), F32)
    return pl.pallas_call(
        functools.partial(_attn_prompt_kernel, seq=s),
        grid=(b,),
        in_specs=[spec],
        out_specs=pl.BlockSpec((1, s, GROUP_WIDTH), lambda i: (i, 0, 0)),
        out_shape=jax.ShapeDtypeStruct((b, s, GROUP_WIDTH), BF16),
        scratch_shapes=[group_buf, group_buf, group_buf, group_buf,
                        pltpu.VMEM((4, 2, 2 * QBLOCK, 2 * QBLOCK), F32),
                        pltpu.VMEM((2, 2 * QBLOCK, 2 * QBLOCK), F32)],
        compiler_params=pltpu.CompilerParams(dimension_semantics=("arbitrary",), vmem_limit_bytes=VMEM_LIMIT),
        name="attention_prompt",
    )(qkv)


def _attn_sample_parts(qkv_ref, c1_ref, c2_ref, c3_ref, o_ref, n1_ref, n2_ref, n3_ref, *, nt):
    nrow = HEADS_PER_GROUP * nt
    nt_shift = nt.bit_length() - 1
    assert nt == 1 << nt_shift
    row_head = lax.broadcasted_iota(jnp.int32, (nrow, GROUP_WIDTH), 0) >> nt_shift
    col_head = lax.broadcasted_iota(jnp.int32, (nrow, GROUP_WIDTH), 1) >> HEAD_SHIFT
    own = row_head == col_head
    col_head_t = lax.broadcasted_iota(jnp.int32, (nt, GROUP_WIDTH), 1) >> HEAD_SHIFT
    tail_lane = lax.broadcasted_iota(jnp.int32, (KV_ROWS, LANES), 1) >= LANES - nt

    def own_blocks(x):
        acc = jnp.zeros((nt, GROUP_WIDTH), F32)
        for h in range(HEADS_PER_GROUP):
            acc = acc + jnp.where(col_head_t == h, x[h * nt:(h + 1) * nt], 0.0)
        return acc

    def group(g):
        c_ref, n_ref = ((c1_ref, n1_ref), (c2_ref, n2_ref), (c3_ref, n3_ref))[g]
        win, dil = ATTN_PATTERNS[g]
        assert c_ref.shape == (1, KV_ROWS, win) and dil & (dil - 1) == 0
        cs = g * GROUP_WIDTH
        k_new = qkv_ref[:, QKV_WIDTH + cs:QKV_WIDTH + cs + GROUP_WIDTH]
        v_new = qkv_ref[:, 2 * QKV_WIDTH + cs:2 * QKV_WIDTH + cs + GROUP_WIDTH]

        new_t = jnp.concatenate([jnp.zeros((LANES - nt, KV_ROWS), F32), jnp.concatenate([k_new, v_new], axis=1)],
                                axis=0).T
        ntiles = win // LANES
        rot = pltpu.roll(c_ref[0, :, 0:LANES], LANES - nt, axis=1)
        for j in range(ntiles):
            nxt = pltpu.roll(c_ref[0, :, (j + 1) * LANES:(j + 2) * LANES], LANES - nt, axis=1) if j + 1 < ntiles else new_t
            n_ref[0, :, j * LANES:(j + 1) * LANES] = jnp.where(tail_lane, nxt, rot)
            rot = nxt

        q = qkv_ref[:, cs:cs + GROUP_WIDTH]
        qbd = jnp.where(own, jnp.concatenate([q] * HEADS_PER_GROUP, axis=0), 0.0).astype(BF16)
        k_old = c_ref[0, 0:GROUP_WIDTH, :].astype(BF16)
        v_old = c_ref[0, GROUP_WIDTH:, :].astype(BF16)
        t_old = lax.broadcasted_iota(jnp.int32, (nrow, win), 0) & (nt - 1)
        i_old = lax.broadcasted_iota(jnp.int32, (nrow, win), 1)
        ok_old = (i_old >= t_old) & (((i_old - t_old) & (dil - 1)) == 0)
        t_new = lax.broadcasted_iota(jnp.int32, (nrow, nt), 0) & (nt - 1)
        i_new = lax.broadcasted_iota(jnp.int32, (nrow, nt), 1)
        ok_new = (i_new <= t_new) & (((t_new - i_new) & (dil - 1)) == 0)
        s_old = _dot(qbd, k_old) + jnp.where(ok_old, 0.0, NEG)
        s_new = _dot_nt(qbd, k_new.astype(BF16)) + jnp.where(ok_new, 0.0, NEG)
        m = jnp.maximum(jnp.max(s_old, axis=-1, keepdims=True), jnp.max(s_new, axis=-1, keepdims=True))
        p_old = jnp.exp2(s_old - m)
        p_new = jnp.exp2(s_new - m)
        den = jnp.sum(p_old, axis=-1, keepdims=True) + jnp.sum(p_new, axis=-1, keepdims=True)
        o = (_dot_nt(p_old.astype(BF16), v_old) + _dot(p_new.astype(BF16), v_new.astype(BF16))) / den
        return own_blocks(o), own_blocks(jnp.broadcast_to(m + jnp.log2(den), (nrow, GROUP_WIDTH)))

    def combine(parts):
        (o0, l0), (o1, l1), (o2, l2) = parts
        m = jnp.maximum(jnp.maximum(l0, l1), l2)
        w0, w1, w2 = jnp.exp2(l0 - m), jnp.exp2(l1 - m), jnp.exp2(l2 - m)
        o_ref[...] = (w0 * o0 + w1 * o1 + w2 * o2) / (w0 + w1 + w2)

    return group, combine


N_MERGE_IN = 8
N_SAMPLE_IN = 4


def _merge_mlp_kernel(*refs, ff_chunk, nt_sample):
    x_ref, gates_ref, at_ref, wpb_ref, wo_ref, ln2_ref, wup_ref, wdn_ref = refs[:N_MERGE_IN]
    y_ref = refs[N_MERGE_IN + (N_SAMPLE_IN if nt_sample else 0)]
    pending = []
    if nt_sample:
        group, combine = _attn_sample_parts(*refs[N_MERGE_IN:N_MERGE_IN + N_SAMPLE_IN],
                                            *refs[N_MERGE_IN + N_SAMPLE_IN + 1:], nt=nt_sample)
        parts = [None] * len(ATTN_PATTERNS)

        def run_group(g):
            parts[g] = group(g)

        def run_combine():
            combine(parts)
        pending = [functools.partial(run_group, g) for g in (2, 1, 0)] + [run_combine]

    def issue_sample_piece():
        if pending:
            pending.pop(0)()

    issue_sample_piece()
    branch_b = _dot(at_ref[...].astype(BF16), wpb_ref[...])
    mixed = (gates_ref[:, 0:D_MODEL].astype(F32) + gates_ref[:, D_MODEL:].astype(F32) * branch_b).astype(BF16)
    h = x_ref[...] + _dot(mixed, wo_ref[...])
    u = _rms_norm_bf16(h, ln2_ref[...])
    acc = h
    for c in range(0, D_FF, ff_chunk):
        z = jnp.maximum(_dot(u, wup_ref[:, c:c + ff_chunk]), 0.0)
        acc = acc + _dot((z * z).astype(BF16), wdn_ref[c:c + ff_chunk, :])
        issue_sample_piece()
    y_ref[...] = acc
    while pending:
        issue_sample_piece()


def _merge_mlp(x2d, gates, attn, w_pb, w_o, ln2, w_up, w_down, sample=None, *, tm=512, ff_chunk=2048):
    m = x2d.shape[0]
    tm = min(tm, m)
    assert m % tm == 0
    row = lambda width: pl.BlockSpec((tm, width), lambda i: (i, 0))
    args = [x2d, gates, attn, w_pb, w_o, ln2, w_up, w_down]
    in_specs = [row(D_MODEL), row(2 * D_MODEL), row(GROUP_WIDTH)] + [_const_spec(w.shape) for w in args[3:]]
    out_specs = [row(D_MODEL)]
    out_shape = [jax.ShapeDtypeStruct((m, D_MODEL), F32)]
    nt = 0
    if sample is not None:
        *sample_args, nt = sample
        qkv, c1, c2, c3 = sample_args
        nb = c1.shape[0]
        assert m // tm == nb and qkv.shape == (nb * nt, 3 * QKV_WIDTH), "one sample batch element per token tile"
        tok = pl.BlockSpec((nt, 3 * QKV_WIDTH), lambda i: (i, 0))
        cache = lambda c: pl.BlockSpec((1,) + c.shape[1:], lambda i: (i, 0, 0))
        args += sample_args
        in_specs += [tok, cache(c1), cache(c2), cache(c3)]
        out_specs += [pl.BlockSpec((nt, GROUP_WIDTH), lambda i: (i, 0)), cache(c1), cache(c2), cache(c3)]
        out_shape += [jax.ShapeDtypeStruct((nb * nt, GROUP_WIDTH), F32)] + [jax.ShapeDtypeStruct(c.shape, F32) for c in (c1, c2, c3)]
    out = pl.pallas_call(
        functools.partial(_merge_mlp_kernel, ff_chunk=ff_chunk, nt_sample=nt),
        grid=(m // tm,),
        in_specs=in_specs,
        out_specs=tuple(out_specs),
        out_shape=tuple(out_shape),
        compiler_params=pltpu.CompilerParams(dimension_semantics=("arbitrary",),
                                             vmem_limit_bytes=VMEM_LIMIT_FUSED if sample is not None else VMEM_LIMIT),
        name="merge_mlp",
    )(*args)
    return out if sample is not None else out[0]


def _position_minor(c):
    n, w = c.shape[1], c.shape[2]
    return jnp.transpose(c[0], (0, 2, 3, 4, 1)).reshape(n, KV_ROWS, w)


def _position_major(c):
    n, _, w = c.shape
    return jnp.transpose(c.reshape(n, 2, HEADS_PER_GROUP, HEAD_DIM, w), (0, 4, 1, 2, 3))[None]


def kernel(x_prompt, x_sample, state_pool, cache_kv1, cache_kv2, cache_kv3, ln1, w_in, q_norm, k_norm,
           pool_lin, pool_scale, w_pa, w_pb, w_o, ln2, w_up, w_down):
    depth = ln1.shape[0]
    assert depth == 1
    b, s, _ = x_prompt.shape
    nb, nt, _ = x_sample.shape

    w_in16, w_pa16, w_pb16 = w_in[0].astype(BF16), w_pa[0].astype(BF16), w_pb[0].astype(BF16)
    w_o16, w_up16, w_dn16 = w_o[0].astype(BF16), w_up[0].astype(BF16), w_down[0].astype(BF16)
    ln1v, ln2v = ln1[0].reshape(1, D_MODEL), ln2[0].reshape(1, D_MODEL)
    qn, kn = q_norm[0].reshape(1, QKV_WIDTH), k_norm[0].reshape(1, QKV_WIDTH)
    scale = pool_scale[0].reshape(1, POOL_WIDTH)
    lin = pool_lin[0].astype(BF16)
    zeros = jnp.zeros((POOL_GROUP_DIM, POOL_GROUP_DIM), BF16)
    lin2 = jnp.stack([jnp.block([[lin[0], zeros], [zeros, lin[1]]]), jnp.block([[lin[2], zeros], [zeros, lin[3]]])])

    qkv, gates, kv1, kv2, kv3, pool_p = _project_prompt(x_prompt, ln1v, w_in16, qn, kn, lin2, scale, w_pa16)
    xs = x_sample.reshape(nb * nt, D_MODEL)
    state_t = jnp.transpose(state_pool[0], (1, 0, 2))
    qkv_s, gates_s, pool_s = _project_sample(xs, state_t, ln1v, w_in16, qn, kn, lin2, scale, w_pa16, nb=nb, nt=nt)

    attn = _attention_prompt(qkv)
    caches = [_position_minor(c) for c in (cache_kv1, cache_kv2, cache_kv3)]
    y_prompt, attn_s, n1, n2, n3 = _merge_mlp(
        x_prompt.reshape(b * s, D_MODEL), gates.reshape(b * s, 2 * D_MODEL), attn.reshape(b * s, GROUP_WIDTH),
        w_pb16, w_o16, ln2v, w_up16, w_dn16, sample=(qkv_s, *caches, nt))
    y_prompt = y_prompt.reshape(b, s, D_MODEL)
    y_sample = _merge_mlp(xs, gates_s, attn_s, w_pb16, w_o16, ln2v, w_up16, w_dn16).reshape(nb, nt, D_MODEL)

    return (y_prompt, y_sample,
            pool_p.reshape(1, b, POOL_STATE, POOL_WIDTH),
            _position_major(kv1), _position_major(kv2), _position_major(kv3),
            jnp.transpose(pool_s, (1, 0, 2))[None],
            _position_major(n1), _position_major(n2), _position_major(n3))
```
